```python
import jax, jax.numpy as jnp
from jax import lax
import numpy as np

D_MODEL = 1024
BATCH = 16
SEQ = 2048
DEPTH = 2
DEC_BATCH = 32
DEC_SEQ = 1
PAST_LEN = 16384
PAGE_SIZE = 128

HEAD_DIM = 64
N_SB_HEADS = 12
SB_WIDTH = N_SB_HEADS * HEAD_DIM
N_MEM_HEADS = 4
MEM_WIDTH = N_MEM_HEADS * HEAD_DIM
N_MEM = 256
GM_GROUPS = 4
GM_WIDTH = 768
GM_GROUP_DIM = GM_WIDTH // GM_GROUPS
CHUNK = 128
Q_BLOCK = 128
MIX_WIDTH = SB_WIDTH + MEM_WIDTH
N_GROUPS = 4
EXPERTS_PER_GROUP = 8
N_EXPERTS = N_GROUPS * EXPERTS_PER_GROUP
TOP_K = 2
D_EXPERT = 256
N_SB_LAYERS = (DEPTH + 1) // 2
N_GM_LAYERS = DEPTH // 2
RMS_EPS = 1e-6
ATTN_SCALE = HEAD_DIM ** -0.5
SB_BIAS_INIT = -6.0

kernel_name = 'hybrid_stickbreak_gmlp_memxattn_hmoe_step'


def rmsnorm(x, g):
    xf = x.astype(jnp.float32)
    y = xf * lax.rsqrt(jnp.mean(xf * xf, axis=-1, keepdims=True) + RMS_EPS)
    return (y * g.astype(jnp.float32)).astype(x.dtype)


def gather_pages(pool, page_table):
    g = pool[page_table]
    return g.reshape(g.shape[0], g.shape[1] * g.shape[2], g.shape[3], g.shape[4])


def sb_block(q, segs, q_pos, bias):
    z = jnp.concatenate([jnp.einsum('bqhd,bkhd->bhqk', q, k) for k, _, _ in segs], axis=-1)
    z = z.astype(jnp.float32) * ATTN_SCALE + bias.astype(jnp.float32)[None, :, None, None]
    k_pos = jnp.concatenate([p for _, _, p in segs])
    mask = k_pos[None, :] < q_pos[:, None]
    log_keep = jnp.where(mask, jax.nn.log_sigmoid(-z), 0.0)
    after = lax.cumsum(log_keep, axis=3, reverse=True) - log_keep
    a = jnp.where(mask, jnp.exp(jax.nn.log_sigmoid(z) + after), 0.0).astype(q.dtype)
    out = None
    off = 0
    for k, v, _ in segs:
        n = k.shape[1]
        part = jnp.einsum('bhqk,bkhd->bqhd', a[..., off:off + n], v)
        out = part if out is None else out + part
        off += n
    return out


def sb_causal(q, segs, q_start, bias):
    n_q = q.shape[1]
    outs = []
    for qs in range(0, n_q, Q_BLOCK):
        qe = min(n_q, qs + Q_BLOCK)
        kend = q_start + qe
        blk = []
        for k, v, k_start in segs:
            n = min(k.shape[1], kend - k_start)
            if n > 0:
                kk = k if n == k.shape[1] else k[:, :n]
                vv = v if n == v.shape[1] else v[:, :n]
                blk.append((kk, vv, k_start + jnp.arange(n)))
        outs.append(sb_block(q[:, qs:qe], blk, q_start + jnp.arange(qs, qe), bias))
    return outs[0] if len(outs) == 1 else jnp.concatenate(outs, axis=1)


def mem_kv(mem, g, w):
    kv = jnp.einsum('bmd,dc->bmc', rmsnorm(mem, g), w)
    b, m, _ = kv.shape
    k = kv[..., :MEM_WIDTH].reshape(b, m, N_MEM_HEADS, HEAD_DIM)
    v = kv[..., MEM_WIDTH:].reshape(b, m, N_MEM_HEADS, HEAD_DIM)
    return k, v


def mem_attend(q, k, v):
    s = jnp.einsum('bqhd,bmhd->bhqm', q, k).astype(jnp.float32) * ATTN_SCALE
    p = jax.nn.softmax(s, axis=-1).astype(v.dtype)
    return jnp.einsum('bhqm,bmhd->bqhd', p, v)


def sb_mixer(h, w_in, sb_bias, w_o, mk, mv, past_segs, q_start):
    b, l, _ = h.shape
    proj = jnp.einsum('bld,dc->blc', h, w_in)
    q = proj[..., :SB_WIDTH].reshape(b, l, N_SB_HEADS, HEAD_DIM)
    k = proj[..., SB_WIDTH:2 * SB_WIDTH].reshape(b, l, N_SB_HEADS, HEAD_DIM)
    v = proj[..., 2 * SB_WIDTH:3 * SB_WIDTH].reshape(b, l, N_SB_HEADS, HEAD_DIM)
    qm = proj[..., 3 * SB_WIDTH:].reshape(b, l, N_MEM_HEADS, HEAD_DIM)
    o_sb = sb_causal(q, past_segs + [(k, v, q_start)], q_start, sb_bias).reshape(b, l, SB_WIDTH)
    o_mem = mem_attend(qm, mk, mv).reshape(b, l, MEM_WIDTH)
    out = jnp.einsum('blc,cd->bld', jnp.concatenate([o_sb, o_mem], axis=-1), w_o)
    return out, k, v


def gm_spatial(v, ws_full, bias):
    b, l, _ = v.shape
    n_chunks = -(-l // CHUNK)
    lp = n_chunks * CHUNK
    vp = jnp.pad(v, ((0, 0), (0, lp - l), (0, 0))).reshape(b, n_chunks, CHUNK, GM_GROUPS, GM_GROUP_DIM)
    mixed = jnp.einsum('gts,bcsgf->bctgf', ws_full.astype(v.dtype), vp)
    mixed = mixed + bias.T.astype(v.dtype)[None, None, :, :, None]
    return mixed.reshape(b, lp, GM_WIDTH)[:, :l]


def gm_mixer(h, w_in, w_o, v_g, ws_full, bias, mk, mv):
    b, l, _ = h.shape
    proj = jnp.einsum('bld,dc->blc', h, w_in)
    u = jax.nn.gelu(proj[..., :GM_WIDTH])
    v = rmsnorm(jax.nn.gelu(proj[..., GM_WIDTH:2 * GM_WIDTH]), v_g)
    qm = proj[..., 2 * GM_WIDTH:].reshape(b, l, N_MEM_HEADS, HEAD_DIM)
    s = u * gm_spatial(v, ws_full, bias)
    o_mem = mem_attend(qm, mk, mv).reshape(b, l, MEM_WIDTH)
    out = jnp.einsum('blc,cd->bld', jnp.concatenate([s, o_mem], axis=-1), w_o)
    return out, v


def hier_moe(h, wg, bg, wr, br, w_gate, w_up, w_down):
    shp = h.shape
    t = h.reshape(-1, D_MODEL)
    n = t.shape[0]
    glog = jnp.dot(t, wg).astype(jnp.float32) + bg.astype(jnp.float32)
    gprob = jax.nn.softmax(glog, axis=-1)
    gsel = jnp.argmax(glog, axis=-1)
    p_grp = jnp.take_along_axis(gprob, gsel[:, None], axis=-1)
    elog = (jnp.dot(t, wr).astype(jnp.float32) + br.astype(jnp.float32)).reshape(n, N_GROUPS, EXPERTS_PER_GROUP)
    elog_sel = jnp.take_along_axis(elog, gsel[:, None, None], axis=1)[:, 0]
    top_v, top_i = lax.top_k(elog_sel, TOP_K)
    w_top = jax.nn.softmax(top_v, axis=-1) * p_grp
    eid = gsel[:, None] * EXPERTS_PER_GROUP + top_i
    gates = jnp.sum(jax.nn.one_hot(eid, N_EXPERTS, dtype=jnp.float32) * w_top[..., None], axis=1).astype(t.dtype)
    y = jnp.zeros_like(t)
    for g in range(N_GROUPS):
        sl = slice(g * EXPERTS_PER_GROUP, (g + 1) * EXPERTS_PER_GROUP)
        a = jnp.einsum('nd,edf->nef', t, w_gate[sl])
        c = jnp.einsum('nd,edf->nef', t, w_up[sl])
        hid = jax.nn.silu(a) * c * gates[:, sl, None]
        y = y + jnp.einsum('nef,efd->nd', hid, w_down[sl])
    return y.reshape(shp)


def setup_inputs(seed: int = 0) -> dict:
    key = jax.random.key(seed)
    ks = jax.random.split(key, 27)
    f32 = jnp.float32
    n_pages = PAST_LEN // PAGE_SIZE
    n_phys = (DEC_BATCH * n_pages * 5) // 4
    n_tril = CHUNK * (CHUNK + 1) // 2

    def nrm(k, shape, scale=1.0):
        return jax.random.normal(k, shape, f32) * scale

    perm = jax.random.permutation(ks[7], n_phys)
    page_table = perm[:DEC_BATCH * n_pages].reshape(DEC_BATCH, n_pages).astype(jnp.int32)
    return {
        'x_prompt': nrm(ks[0], (BATCH, SEQ, D_MODEL)),
        'x_sample': nrm(ks[1], (DEC_BATCH, DEC_SEQ, D_MODEL)),
        'mem_prompt': nrm(ks[2], (BATCH, N_MEM, D_MODEL)),
        'cache_sb_k': nrm(ks[3], (N_SB_LAYERS, n_phys, PAGE_SIZE, N_SB_HEADS, HEAD_DIM)),
        'cache_sb_v': nrm(ks[4], (N_SB_LAYERS, n_phys, PAGE_SIZE, N_SB_HEADS, HEAD_DIM)),
        'cache_mem_k': nrm(ks[5], (DEPTH, DEC_BATCH, N_MEM, N_MEM_HEADS, HEAD_DIM)),
        'cache_mem_v': nrm(ks[6], (DEPTH, DEC_BATCH, N_MEM, N_MEM_HEADS, HEAD_DIM)),
        'page_table': page_table,
        'norm_mix_g': 1.0 + nrm(ks[8], (DEPTH, D_MODEL), 0.02),
        'norm_ffn_g': 1.0 + nrm(ks[9], (DEPTH, D_MODEL), 0.02),
        'norm_mem_g': 1.0 + nrm(ks[10], (DEPTH, D_MODEL), 0.02),
        'w_mem_kv': nrm(ks[11], (DEPTH, D_MODEL, 2 * MEM_WIDTH), D_MODEL ** -0.5),
        'w_in_sb': nrm(ks[12], (N_SB_LAYERS, D_MODEL, 3 * SB_WIDTH + MEM_WIDTH), D_MODEL ** -0.5),
        'sb_bias': SB_BIAS_INIT + nrm(ks[26], (N_SB_LAYERS, N_SB_HEADS), 0.1),
        'w_in_gm': nrm(ks[13], (N_GM_LAYERS, D_MODEL, 2 * GM_WIDTH + MEM_WIDTH), D_MODEL ** -0.5),
        'gm_norm_g': 1.0 + nrm(ks[14], (N_GM_LAYERS, GM_WIDTH), 0.02),
        'gm_ws_tril': nrm(ks[15], (N_GM_LAYERS, GM_GROUPS, n_tril), CHUNK ** -0.5),
        'gm_b': 1.0 + nrm(ks[16], (N_GM_LAYERS, GM_GROUPS, CHUNK), 0.02),
        'w_out': nrm(ks[17], (DEPTH, MIX_WIDTH, D_MODEL), MIX_WIDTH ** -0.5),
        'w_router_grp': nrm(ks[18], (DEPTH, D_MODEL, N_GROUPS), D_MODEL ** -0.5),
        'b_router_grp': nrm(ks[19], (DEPTH, N_GROUPS), 0.01),
        'w_router_exp': nrm(ks[20], (DEPTH, D_MODEL, N_EXPERTS), D_MODEL ** -0.5),
        'b_router_exp': nrm(ks[21], (DEPTH, N_EXPERTS), 0.01),
        'w_gate': nrm(ks[22], (DEPTH, N_EXPERTS, D_MODEL, D_EXPERT), D_MODEL ** -0.5),
        'w_up': nrm(ks[23], (DEPTH, N_EXPERTS, D_MODEL, D_EXPERT), D_MODEL ** -0.5),
        'w_down': nrm(ks[24], (DEPTH, N_EXPERTS, D_EXPERT, D_MODEL), D_EXPERT ** -0.5),
        'final_norm_g': 1.0 + nrm(ks[25], (D_MODEL,), 0.02),
    }


def reference(x_prompt, x_sample, mem_prompt, cache_sb_k, cache_sb_v, cache_mem_k, cache_mem_v, page_table,
              norm_mix_g, norm_ffn_g, norm_mem_g, w_mem_kv, w_in_sb, sb_bias, w_in_gm, gm_norm_g, gm_ws_tril,
              gm_b, w_out, w_router_grp, b_router_grp, w_router_exp, b_router_exp, w_gate, w_up, w_down,
              final_norm_g):
    rows, cols = np.tril_indices(CHUNK)
    xp, xs = x_prompt, x_sample
    sbk_p, sbv_p, sbk_s, sbv_s, memk_p, memv_p, gmv_s = [], [], [], [], [], [], []
    for l in range(DEPTH):
        mk_p, mv_p = mem_kv(mem_prompt, norm_mem_g[l], w_mem_kv[l])
        memk_p.append(mk_p)
        memv_p.append(mv_p)
        mk_s, mv_s = cache_mem_k[l], cache_mem_v[l]
        hp = rmsnorm(xp, norm_mix_g[l])
        hs = rmsnorm(xs, norm_mix_g[l])
        i = l // 2
        if l % 2 == 0:
            past_k = gather_pages(cache_sb_k[i], page_table)
            past_v = gather_pages(cache_sb_v[i], page_table)
            op, kp, vp = sb_mixer(hp, w_in_sb[i], sb_bias[i], w_out[l], mk_p, mv_p, [], 0)
            os_, ks_, vs_ = sb_mixer(hs, w_in_sb[i], sb_bias[i], w_out[l], mk_s, mv_s,
                                     [(past_k, past_v, 0)], PAST_LEN)
            sbk_p.append(kp)
            sbv_p.append(vp)
            sbk_s.append(ks_)
            sbv_s.append(vs_)
        else:
            ws_full = jnp.zeros((GM_GROUPS, CHUNK, CHUNK), gm_ws_tril.dtype).at[:, rows, cols].set(gm_ws_tril[i])
            op, _ = gm_mixer(hp, w_in_gm[i], w_out[l], gm_norm_g[i], ws_full, gm_b[i], mk_p, mv_p)
            os_, v_new = gm_mixer(hs, w_in_gm[i], w_out[l], gm_norm_g[i], ws_full, gm_b[i], mk_s, mv_s)
            gmv_s.append(v_new)
        xp = xp + op
        xs = xs + os_
        xp = xp + hier_moe(rmsnorm(xp, norm_ffn_g[l]), w_router_grp[l], b_router_grp[l], w_router_exp[l],
                           b_router_exp[l], w_gate[l], w_up[l], w_down[l])
        xs = xs + hier_moe(rmsnorm(xs, norm_ffn_g[l]), w_router_grp[l], b_router_grp[l], w_router_exp[l],
                           b_router_exp[l], w_gate[l], w_up[l], w_down[l])
    y_prompt = rmsnorm(xp, final_norm_g)
    y_sample = rmsnorm(xs, final_norm_g)
    new_sb_k_prompt = jnp.stack(sbk_p)
    new_sb_v_prompt = jnp.stack(sbv_p)
    new_sb_k_sample = jnp.stack(sbk_s)
    new_sb_v_sample = jnp.stack(sbv_s)
    new_mem_k_prompt = jnp.stack(memk_p)
    new_mem_v_prompt = jnp.stack(memv_p)
    new_gm_v_sample = jnp.stack(gmv_s)
    return (y_prompt, y_sample, new_sb_k_prompt, new_sb_v_prompt, new_sb_k_sample, new_sb_v_sample,
            new_mem_k_prompt, new_mem_v_prompt, new_gm_v_sample)
```

```python
import functools

import numpy as np
import jax
import jax.numpy as jnp
from jax import lax
from jax.experimental import pallas as pl
from jax.experimental.pallas import tpu as pltpu

F32 = jnp.float32
BF16 = jnp.bfloat16
I32 = jnp.int32

HEAD_DIM = 64
LANES = 128
SUBLANES = 8
N_SB_HEADS = 12
SB_WIDTH = N_SB_HEADS * HEAD_DIM
N_MEM_HEADS = 4
MEM_WIDTH = N_MEM_HEADS * HEAD_DIM
GM_GROUPS = 4
GM_WIDTH = 768
GM_GROUP_DIM = GM_WIDTH // GM_GROUPS
CHUNK = 128
PAGE_SIZE = 128
N_GROUPS = 4
EXPERTS_PER_GROUP = 8
N_EXPERTS = N_GROUPS * EXPERTS_PER_GROUP
TOP_K = 2
RMS_EPS = 1e-6
ATTN_SCALE = HEAD_DIM ** -0.5

SB_TQ = 128
SB_TK = 128
PAGES_PER_STEP = 8
SAMPLE_ROWS = 2 * SUBLANES
VMEM_LIMIT = 48 * 1024 * 1024
NT_DIMS = (((1,), (1,)), ((), ()))


def _cparams(n_axes):
    return pltpu.CompilerParams(dimension_semantics=("arbitrary",) * n_axes,
                                vmem_limit_bytes=VMEM_LIMIT)


def _mm(a, b):
    return jnp.dot(a.astype(BF16), b.astype(BF16), preferred_element_type=F32)


def _mm_nt(a, b):
    return lax.dot_general(a.astype(BF16), b.astype(BF16), NT_DIMS, preferred_element_type=F32)


def _split_bf16(x):
    hi = x.astype(BF16)
    lo = (x - hi.astype(F32)).astype(BF16)
    return hi, lo


def _rms(x, g):
    return x * lax.rsqrt(jnp.mean(x * x, axis=-1, keepdims=True) + RMS_EPS) * g


def _gelu(x):
    return x * (0.5 * (1.0 + jnp.tanh(np.sqrt(2.0 / np.pi).astype(np.float32) * (x + 0.044715 * (x * x * x)))))


def _log_sigmoid_pair(z):
    l = jnp.log1p(jnp.exp(-jnp.abs(z)))
    return jnp.minimum(z, 0.0) - l, -(jnp.maximum(z, 0.0) + l)


def _cumsum_weights():
    j = np.arange(SB_TK)[:, None]
    s = np.arange(SB_TK)[None, :]
    w = np.concatenate([(j > s).astype(np.float32), np.ones((SB_TK, SB_TK), np.float32)], axis=1)
    return jnp.asarray(np.concatenate([w, w], axis=0), dtype=BF16)


def _later_sums(log_keep, w2):
    hi, lo = _split_bf16(log_keep)
    return jnp.dot(jnp.concatenate([hi, lo], axis=1), w2, preferred_element_type=F32)


def _norm_matmul_kernel(x_ref, g_ref, *refs, n_out):
    h = _rms(x_ref[...], g_ref[...]).astype(BF16)
    for w_ref, o_ref in zip(refs[:n_out], refs[n_out:]):
        o_ref[...] = jnp.dot(h, w_ref[...], preferred_element_type=F32)


def norm_matmul(x, g, ws, tm):
    n, d = x.shape
    tm = min(tm, n)
    assert n % tm == 0
    ws = [w.astype(BF16) for w in ws]
    in_specs = [pl.BlockSpec((tm, d), lambda i: (i, 0)), pl.BlockSpec((1, d), lambda i: (0, 0))]
    in_specs += [pl.BlockSpec(w.shape, lambda i: (0, 0)) for w in ws]
    out_specs = [pl.BlockSpec((tm, w.shape[1]), lambda i: (i, 0)) for w in ws]
    out_shape = [jax.ShapeDtypeStruct((n, w.shape[1]), F32) for w in ws]
    return pl.pallas_call(
        functools.partial(_norm_matmul_kernel, n_out=len(ws)),
        grid=(n // tm,), in_specs=in_specs, out_specs=out_specs, out_shape=out_shape,
        compiler_params=_cparams(1))(x, g.reshape(1, d), *ws)


def _rmsnorm_kernel(x_ref, g_ref, o_ref):
    o_ref[...] = _rms(x_ref[...], g_ref[...])


def rmsnorm_rows(x, g, tm):
    n, d = x.shape
    tm = min(tm, n)
    assert n % tm == 0
    return pl.pallas_call(
        _rmsnorm_kernel, grid=(n // tm,),
        in_specs=[pl.BlockSpec((tm, d), lambda i: (i, 0)), pl.BlockSpec((1, d), lambda i: (0, 0))],
        out_specs=pl.BlockSpec((tm, d), lambda i: (i, 0)),
        out_shape=jax.ShapeDtypeStruct((n, d), F32), compiler_params=_cparams(1))(x, g.reshape(1, d))


def _sb_prompt_kernel(bias_ref, q_ref, k_ref, v_ref, w2_ref, o_ref, kb_ref, vb_ref):
    hp = pl.program_id(1)
    qi = pl.program_id(2)

    @pl.when(qi == 0)
    def _():
        kb_ref[...] = k_ref[...].astype(BF16)
        vb_ref[...] = v_ref[...].astype(BF16)

    lane = lax.broadcasted_iota(I32, (SB_TQ, LANES), 1)
    head_masks = (lane < HEAD_DIM, lane >= HEAD_DIM)
    q = q_ref[...] * ATTN_SCALE
    qh = [jnp.where(m, q, 0.0).astype(BF16) for m in head_masks]
    bias = [bias_ref[2 * hp], bias_ref[2 * hp + 1]]
    w2 = w2_ref[...]
    row = lax.broadcasted_iota(I32, (SB_TQ, SB_TK), 0)
    col = lax.broadcasted_iota(I32, (SB_TQ, SB_TK), 1)
    strictly_before = col < row

    def block(j, carry, diagonal):
        start = pl.multiple_of(j * SB_TK, SB_TK)
        kj = kb_ref[pl.ds(start, SB_TK), :]
        vj = vb_ref[pl.ds(start, SB_TK), :]
        out = []
        for h in range(2):
            acc, later = carry[h]
            z = lax.dot_general(qh[h], kj, NT_DIMS, preferred_element_type=F32) + bias[h]
            log_beta, log_keep = _log_sigmoid_pair(z)
            if diagonal:
                log_keep = jnp.where(strictly_before, log_keep, 0.0)
            sums = _later_sums(log_keep, w2)
            a = jnp.exp(log_beta + sums[:, :SB_TK] + later)
            if diagonal:
                a = jnp.where(strictly_before, a, 0.0)
            acc = acc + jnp.dot(a.astype(BF16), vj, preferred_element_type=F32)
            out.append((acc, later + sums[:, SB_TK:]))
        return tuple(out)

    zero = jnp.zeros((SB_TQ, LANES), F32)
    carry = block(qi, ((zero, zero), (zero, zero)), True)
    carry = lax.fori_loop(0, qi, lambda t, c: block(qi - 1 - t, c, False), carry)
    o_ref[...] = jnp.where(head_masks[0], carry[0][0], carry[1][0])


def sb_attention_prompt(q, k, v, bias, batch, seq):
    n = q.shape[0]
    assert seq % SB_TQ == 0 and SB_TQ == SB_TK and n == batch * seq
    nq = seq // SB_TQ
    return pl.pallas_call(
        _sb_prompt_kernel,
        grid=(batch, N_SB_HEADS // 2, nq),
        in_specs=[
            pl.BlockSpec(memory_space=pltpu.SMEM),
            pl.BlockSpec((SB_TQ, LANES), lambda b, hp, qi: (b * nq + qi, hp)),
            pl.BlockSpec((seq, LANES), lambda b, hp, qi: (b, hp)),
            pl.BlockSpec((seq, LANES), lambda b, hp, qi: (b, hp)),
            pl.BlockSpec((2 * SB_TK, 2 * SB_TK), lambda b, hp, qi: (0, 0)),
        ],
        out_specs=pl.BlockSpec((SB_TQ, LANES), lambda b, hp, qi: (b * nq + qi, hp)),
        out_shape=jax.ShapeDtypeStruct((n, SB_WIDTH), F32),
        scratch_shapes=[pltpu.VMEM((seq, LANES), BF16), pltpu.VMEM((seq, LANES), BF16)],
        compiler_params=_cparams(3))(bias, q, k, v, _cumsum_weights())


SB_HEAD_ROWS = 16


def _sb_sample_kernel(pt_ref, q_ref, bias_ref, w2_ref, *refs):
    del pt_ref
    k_refs = refs[:PAGES_PER_STEP]
    v_refs = refs[PAGES_PER_STEP:2 * PAGES_PER_STEP]
    o_ref = refs[2 * PAGES_PER_STEP]
    acc_ref, later_ref = refs[2 * PAGES_PER_STEP + 1:]
    j = pl.program_id(1)

    @pl.when(j == 0)
    def _():
        acc_ref[...] = jnp.zeros_like(acc_ref)
        later_ref[...] = jnp.zeros_like(later_ref)

    head = lax.broadcasted_iota(I32, (SB_HEAD_ROWS, SB_WIDTH), 0)
    chan_head = lax.broadcasted_iota(I32, (SB_HEAD_ROWS, SB_WIDTH), 1) // HEAD_DIM
    own = head == chan_head
    q_rows = jnp.where(own, q_ref[0] * ATTN_SCALE, 0.0).astype(BF16)
    bias = bias_ref[...]
    w2 = w2_ref[...]

    acc = acc_ref[...]
    later = later_ref[...]
    for i in range(PAGES_PER_STEP):
        z = lax.dot_general(q_rows, k_refs[i][0].astype(BF16), NT_DIMS, preferred_element_type=F32) + bias
        log_beta, log_keep = _log_sigmoid_pair(z)
        sums = _later_sums(log_keep, w2)
        a = jnp.exp(log_beta + sums[:, :PAGE_SIZE] + later)
        acc = acc + jnp.dot(a.astype(BF16), v_refs[i][0].astype(BF16), preferred_element_type=F32)
        later = later + sums[:, PAGE_SIZE:]
    acc_ref[...] = acc
    later_ref[...] = later

    @pl.when(j == pl.num_programs(1) - 1)
    def _():
        o_ref[0] = jnp.sum(jnp.where(own, acc, 0.0), axis=0, keepdims=True)


def sb_attention_sample(q, cache_k, cache_v, page_table, bias):
    nb, n_pages = page_table.shape
    assert n_pages % PAGES_PER_STEP == 0 and PAGE_SIZE == SB_TK
    steps = n_pages // PAGES_PER_STEP
    bias_rows = jnp.zeros((SB_HEAD_ROWS, PAGE_SIZE), F32).at[:N_SB_HEADS].set(
        jnp.broadcast_to(bias[:, None], (N_SB_HEADS, PAGE_SIZE)))

    def page_spec(i):
        return pl.BlockSpec((1, PAGE_SIZE, SB_WIDTH),
                            lambda b, j, pt: (pt[b, n_pages - 1 - (j * PAGES_PER_STEP + i)], 0, 0))

    grid_spec = pltpu.PrefetchScalarGridSpec(
        num_scalar_prefetch=1, grid=(nb, steps),
        in_specs=[pl.BlockSpec((1, 1, SB_WIDTH), lambda b, j, pt: (b, 0, 0)),
                  pl.BlockSpec((SB_HEAD_ROWS, PAGE_SIZE), lambda b, j, pt: (0, 0)),
                  pl.BlockSpec((2 * SB_TK, 2 * SB_TK), lambda b, j, pt: (0, 0))]
        + [page_spec(i) for i in range(PAGES_PER_STEP)] * 2,
        out_specs=pl.BlockSpec((1, 1, SB_WIDTH), lambda b, j, pt: (b, 0, 0)),
        scratch_shapes=[pltpu.VMEM((SB_HEAD_ROWS, SB_WIDTH), F32), pltpu.VMEM((SB_HEAD_ROWS, PAGE_SIZE), F32)])
    out = pl.pallas_call(
        _sb_sample_kernel, grid_spec=grid_spec,
        out_shape=jax.ShapeDtypeStruct((nb, 1, SB_WIDTH), F32),
        compiler_params=_cparams(2))(
            page_table, q.reshape(nb, 1, SB_WIDTH), bias_rows, _cumsum_weights(),
            *([cache_k] * PAGES_PER_STEP), *([cache_v] * PAGES_PER_STEP))
    return out.reshape(nb, SB_WIDTH)


def _gm_kernel(pu_ref, pv_ref, vg_ref, ws_ref, bias_ref, s_ref, v_ref):
    u = _gelu(pu_ref[...])
    v = _rms(_gelu(pv_ref[...]), vg_ref[...])
    v_ref[...] = v
    group = lax.broadcasted_iota(I32, v.shape, 1) // GM_GROUP_DIM
    vb = v.astype(BF16)
    mixed = bias_ref[...]
    for g in range(GM_GROUPS):
        mixed = mixed + jnp.where(group == g, jnp.dot(ws_ref[g], vb, preferred_element_type=F32), 0.0)
    s_ref[...] = u * mixed


def gm_gate(pu, pv, vg, ws_full, bias, chunk):
    n = pu.shape[0]
    assert n % chunk == 0
    bias_full = jnp.repeat(bias.T, GM_GROUP_DIM, axis=1)
    row = lambda i: (i, 0)
    return pl.pallas_call(
        _gm_kernel, grid=(n // chunk,),
        in_specs=[pl.BlockSpec((chunk, GM_WIDTH), row), pl.BlockSpec((chunk, GM_WIDTH), row),
                  pl.BlockSpec((1, GM_WIDTH), lambda i: (0, 0)),
                  pl.BlockSpec((GM_GROUPS, chunk, chunk), lambda i: (0, 0, 0)),
                  pl.BlockSpec((chunk, GM_WIDTH), lambda i: (0, 0))],
        out_specs=[pl.BlockSpec((chunk, GM_WIDTH), row), pl.BlockSpec((chunk, GM_WIDTH), row)],
        out_shape=[jax.ShapeDtypeStruct((n, GM_WIDTH), F32)] * 2,
        compiler_params=_cparams(1))(pu, pv, vg.reshape(1, GM_WIDTH), ws_full.astype(BF16), bias_full)


def _mixer_out_kernel(x_ref, o_ref, qm_ref, mk_ref, mv_ref, wo_ref, out_ref):
    qm = qm_ref[...] * ATTN_SCALE
    mk = mk_ref[0].astype(BF16)
    mv = mv_ref[0].astype(BF16)
    head = lax.broadcasted_iota(I32, qm.shape, 1) // HEAD_DIM
    o_mem = jnp.zeros(qm.shape, F32)
    for h in range(N_MEM_HEADS):
        own = head == h
        s = _mm_nt(jnp.where(own, qm, 0.0), mk)
        p = jnp.exp(s - jnp.max(s, axis=-1, keepdims=True))
        p = p / jnp.sum(p, axis=-1, keepdims=True)
        o_mem = jnp.where(own, _mm(p, mv), o_mem)
    wo = wo_ref[...]
    width = o_ref.shape[1]
    out_ref[...] = x_ref[...] + _mm(o_ref[...], wo[:width]) + _mm(o_mem, wo[width:])


def mixer_out(x, o_mix, qm, mk, mv, wo, rows_per_batch, tm):
    n, d = x.shape
    tm = min(tm, rows_per_batch)
    assert rows_per_batch % tm == 0 and n % tm == 0
    per = rows_per_batch // tm
    wo = wo.astype(BF16)
    row = lambda i: (i, 0)
    mem = lambda i: (i // per, 0, 0)
    return pl.pallas_call(
        _mixer_out_kernel, grid=(n // tm,),
        in_specs=[pl.BlockSpec((tm, d), row), pl.BlockSpec((tm, o_mix.shape[1]), row),
                  pl.BlockSpec((tm, MEM_WIDTH), row),
                  pl.BlockSpec((1,) + mk.shape[1:], mem), pl.BlockSpec((1,) + mv.shape[1:], mem),
                  pl.BlockSpec(wo.shape, lambda i: (0, 0))],
        out_specs=pl.BlockSpec((tm, d), row),
        out_shape=jax.ShapeDtypeStruct((n, d), F32),
        compiler_params=_cparams(1))(x, o_mix, qm, mk, mv, wo)


ROUTE_LANES = LANES


def _route_kernel(x_ref, g_ref, w_ref, b_ref, t_ref, rw_ref, re_ref):
    t = _rms(x_ref[...], g_ref[...])
    t_ref[...] = t
    logits = _mm(t, w_ref[...]) + b_ref[...]
    lane = lax.broadcasted_iota(I32, logits.shape, 1).astype(F32)
    neg = jnp.float32(-jnp.inf)
    far = jnp.float32(ROUTE_LANES)

    def first_max(vals):
        m = jnp.max(vals, axis=-1, keepdims=True)
        return m, jnp.min(jnp.where(vals == m, lane, far), axis=-1, keepdims=True)

    is_group = lane < N_GROUPS
    gmax, gsel = first_max(jnp.where(is_group, logits, neg))
    p_grp = 1.0 / jnp.sum(jnp.where(is_group, jnp.exp(logits - gmax), 0.0), axis=-1, keepdims=True)
    lo = N_GROUPS + gsel * EXPERTS_PER_GROUP
    cand = jnp.where((lane >= lo) & (lane < lo + EXPERTS_PER_GROUP), logits, neg)
    m1, i1 = first_max(cand)
    m2, i2 = first_max(jnp.where(lane == i1, neg, cand))
    e2 = jnp.exp(m2 - m1)
    den = 1.0 + e2
    w1 = (1.0 / den) * p_grp
    w2 = (e2 / den) * p_grp
    rw_ref[...] = jnp.where(lane == 0, w1, jnp.where(lane == 1, w2, 0.0))
    re_ref[...] = jnp.where(lane == 0, i1 - N_GROUPS, jnp.where(lane == 1, i2 - N_GROUPS, 0.0)).astype(I32)


def moe_route(x, g, wg, bg, wr, br, tm):
    n, d = x.shape
    tm = min(tm, n)
    assert n % tm == 0
    pad = ROUTE_LANES - N_GROUPS - N_EXPERTS
    w = jnp.concatenate([wg, wr, jnp.zeros((d, pad), F32)], axis=1).astype(BF16)
    b = jnp.concatenate([bg, br, jnp.zeros((pad,), F32)]).reshape(1, ROUTE_LANES)
    row = lambda i: (i, 0)
    return pl.pallas_call(
        _route_kernel, grid=(n // tm,),
        in_specs=[pl.BlockSpec((tm, d), row), pl.BlockSpec((1, d), lambda i: (0, 0)),
                  pl.BlockSpec((d, ROUTE_LANES), lambda i: (0, 0)), pl.BlockSpec((1, ROUTE_LANES), lambda i: (0, 0))],
        out_specs=[pl.BlockSpec((tm, d), row), pl.BlockSpec((tm, ROUTE_LANES), row),
                   pl.BlockSpec((tm, ROUTE_LANES), row)],
        out_shape=[jax.ShapeDtypeStruct((n, d), F32), jax.ShapeDtypeStruct((n, ROUTE_LANES), F32),
                   jax.ShapeDtypeStruct((n, ROUTE_LANES), I32)],
        compiler_params=_cparams(1))(x, g.reshape(1, d), w, b)


def _row_copy(src_hbm, row, buf, slot, r, sem):
    return pltpu.make_async_copy(src_hbm.at[pl.ds(row, 1)], buf.at[slot, pl.ds(r, 1)], sem.at[slot])


def _gather_rows_pipelined(idx_ref, nxt_ref, src_hbm, buf, sem, count):
    i = pl.program_id(0)
    slot = i % 2

    def issue(ref, s):
        def body(r, c):
            _row_copy(src_hbm, ref[0, 0, r], buf, s, r, sem).start()
            return c
        lax.fori_loop(0, count, body, 0)

    @pl.when(i == 0)
    def _():
        issue(idx_ref, 0)

    @pl.when(i + 1 < pl.num_programs(0))
    def _():
        issue(nxt_ref, 1 - slot)

    pltpu.make_async_copy(src_hbm.at[pl.ds(0, count)], buf.at[slot], sem.at[slot]).wait()
    return slot


def _gather_kernel(idx_ref, nxt_ref, src_hbm, out_ref, buf, sem, *, rows):
    slot = _gather_rows_pipelined(idx_ref, nxt_ref, src_hbm, buf, sem, rows)
    out_ref[...] = buf[slot].astype(out_ref.dtype)


def _tile_index_specs(nt, width):
    cur = pl.BlockSpec((1, 1, width), lambda i: (i, 0, 0), memory_space=pltpu.SMEM)
    nxt = pl.BlockSpec((1, 1, width), lambda i: (jnp.minimum(i + 1, nt - 1), 0, 0), memory_space=pltpu.SMEM)
    return [cur, nxt]


def gather_rows(src, idx, rows, out_dtype):
    p = idx.shape[0]
    d = src.shape[1]
    assert p % rows == 0 and src.shape[0] >= rows
    nt = p // rows
    idx3 = idx.reshape(nt, 1, rows)
    return pl.pallas_call(
        functools.partial(_gather_kernel, rows=rows), grid=(nt,),
        in_specs=_tile_index_specs(nt, rows) + [pl.BlockSpec(memory_space=pl.ANY)],
        out_specs=pl.BlockSpec((rows, d), lambda i: (i, 0)),
        out_shape=jax.ShapeDtypeStruct((p, d), out_dtype),
        scratch_shapes=[pltpu.VMEM((2, rows, d), src.dtype), pltpu.SemaphoreType.DMA((2,))],
        compiler_params=_cparams(1))(idx3, idx3, src)


def _expert_kernel(te_ref, used_ref, t_ref, gate_ref, wg_ref, wu_ref, wd_ref, y_ref, wg, wu, wd):
    i = pl.program_id(0)
    prev = te_ref[jnp.maximum(i - 1, 0)]

    @pl.when((i == 0) | (te_ref[i] != prev))
    def _():
        wg[...] = wg_ref[0].astype(BF16)
        wu[...] = wu_ref[0].astype(BF16)
        wd[...] = wd_ref[0].astype(BF16)

    @pl.when(i < used_ref[0])
    def _():
        t = t_ref[...]
        a = jnp.dot(t, wg[...], preferred_element_type=F32)
        c = jnp.dot(t, wu[...], preferred_element_type=F32)
        hid = a * (1.0 / (1.0 + jnp.exp(-a))) * c * gate_ref[...]
        y_ref[...] = _mm(hid, wd[...])

    @pl.when(i >= used_ref[0])
    def _():
        y_ref[...] = jnp.zeros_like(y_ref)


def moe_experts(t_sorted, gate_sorted, tile_expert, n_used, w_gate, w_up, w_down, tm):
    p, d = t_sorted.shape
    f = w_gate.shape[2]
    assert p % tm == 0
    grid_spec = pltpu.PrefetchScalarGridSpec(
        num_scalar_prefetch=2, grid=(p // tm,),
        in_specs=[pl.BlockSpec((tm, d), lambda i, te, nu: (i, 0)),
                  pl.BlockSpec((tm, 1), lambda i, te, nu: (i, 0)),
                  pl.BlockSpec((1, d, f), lambda i, te, nu: (te[i], 0, 0)),
                  pl.BlockSpec((1, d, f), lambda i, te, nu: (te[i], 0, 0)),
                  pl.BlockSpec((1, f, d), lambda i, te, nu: (te[i], 0, 0))],
        out_specs=pl.BlockSpec((tm, d), lambda i, te, nu: (i, 0)),
        scratch_shapes=[pltpu.VMEM((d, f), BF16), pltpu.VMEM((d, f), BF16), pltpu.VMEM((f, d), BF16)])
    return pl.pallas_call(
        _expert_kernel, grid_spec=grid_spec,
        out_shape=jax.ShapeDtypeStruct((p, d), F32),
        compiler_params=_cparams(1))(tile_expert, n_used, t_sorted, gate_sorted, w_gate, w_up, w_down)


def _combine_kernel(idx_ref, nxt_ref, x_ref, y_hbm, out_ref, buf, sem, *, rows):
    slot = _gather_rows_pipelined(idx_ref, nxt_ref, y_hbm, buf, sem, TOP_K * rows)
    out_ref[...] = x_ref[...] + buf[slot, :rows] + buf[slot, rows:]


def moe_combine(x, y_sorted, pos, rows):
    n, d = x.shape
    assert n % rows == 0 and y_sorted.shape[0] >= TOP_K * rows
    nt = n // rows
    idx3 = pos.reshape(nt, rows, TOP_K).transpose(0, 2, 1).reshape(nt, 1, TOP_K * rows)
    return pl.pallas_call(
        functools.partial(_combine_kernel, rows=rows), grid=(nt,),
        in_specs=_tile_index_specs(nt, TOP_K * rows)
        + [pl.BlockSpec((rows, d), lambda i: (i, 0)), pl.BlockSpec(memory_space=pl.ANY)],
        out_specs=pl.BlockSpec((rows, d), lambda i: (i, 0)),
        out_shape=jax.ShapeDtypeStruct((n, d), F32),
        scratch_shapes=[pltpu.VMEM((2, TOP_K * rows, d), F32), pltpu.SemaphoreType.DMA((2,))],
        compiler_params=_cparams(1))(idx3, idx3, x, y_sorted)


def _moe_plan(route_e, route_w, tm):
    n = route_e.shape[0]
    e = route_e[:, :TOP_K].reshape(-1)
    w = route_w[:, :TOP_K].reshape(-1)
    na = e.shape[0]
    order = jnp.argsort(e, stable=True).astype(I32)
    e_sorted = e[order]
    counts = jnp.zeros((N_EXPERTS,), I32).at[e].add(1)
    padded = ((counts + tm - 1) // tm) * tm
    seg_start = jnp.cumsum(counts) - counts
    pad_end = jnp.cumsum(padded)
    pad_start = pad_end - padded
    pos_sorted = pad_start[e_sorted] + jnp.arange(na, dtype=I32) - seg_start[e_sorted]
    total = ((na + tm - 1) // tm) * tm + N_EXPERTS * tm
    row_token = jnp.zeros((total,), I32).at[pos_sorted].set(order // TOP_K)
    row_gate = jnp.zeros((total,), F32).at[pos_sorted].set(w[order])
    pos = jnp.zeros((na,), I32).at[order].set(pos_sorted).reshape(n, TOP_K)
    tile_start = jnp.arange(total // tm, dtype=I32) * tm
    tile_expert = jnp.minimum(jnp.searchsorted(pad_end, tile_start, side="right"), N_EXPERTS - 1).astype(I32)
    n_used = (pad_end[-1:] // tm).astype(I32)
    return row_token, row_gate.reshape(total, 1), pos, tile_expert, n_used


def hier_moe_residual(x, g, wg, bg, wr, br, w_gate, w_up, w_down, tm_tokens, tm_expert):
    t, route_w, route_e = moe_route(x, g, wg, bg, wr, br, tm_tokens)
    row_token, row_gate, pos, tile_expert, n_used = _moe_plan(route_e, route_w, tm_expert)
    t_sorted = gather_rows(t, row_token, tm_expert, BF16)
    y_sorted = moe_experts(t_sorted, row_gate, tile_expert, n_used, w_gate, w_up, w_down, tm_expert)
    return moe_combine(x, y_sorted, pos, min(tm_expert, x.shape[0]))


def _pad_rows(x):
    b, c = x.shape
    return jnp.zeros((b, SAMPLE_ROWS, c), x.dtype).at[:, 0].set(x).reshape(b * SAMPLE_ROWS, c)


def _unpad_rows(x):
    return x.reshape(-1, SAMPLE_ROWS, x.shape[1])[:, 0]


def kernel(x_prompt, x_sample, mem_prompt, cache_sb_k, cache_sb_v, cache_mem_k, cache_mem_v, page_table, norm_mix_g, norm_ffn_g, norm_mem_g, w_mem_kv, w_in_sb, sb_bias, w_in_gm, gm_norm_g, gm_ws_tril, gm_b, w_out, w_router_grp, b_router_grp, w_router_exp, b_router_exp, w_gate, w_up, w_down, final_norm_g):
    batch, seq, d = x_prompt.shape
    dec_batch, dec_seq, _ = x_sample.shape
    assert dec_seq == 1
    n_mem = mem_prompt.shape[1]
    depth = w_out.shape[0]
    n_phys = cache_sb_k.shape[1]
    rows, cols = np.tril_indices(CHUNK)

    xp = x_prompt.reshape(batch * seq, d)
    xs = x_sample.reshape(dec_batch, d)
    memp = mem_prompt.reshape(batch * n_mem, d)
    sbk_p, sbv_p, sbk_s, sbv_s, memk_p, memv_p, gmv_s = [], [], [], [], [], [], []

    for l in range(depth):
        i = l // 2
        mk_p, mv_p = norm_matmul(memp, norm_mem_g[l], [w_mem_kv[l][:, :MEM_WIDTH], w_mem_kv[l][:, MEM_WIDTH:]], 512)
        memk_p.append(mk_p.reshape(batch, n_mem, N_MEM_HEADS, HEAD_DIM))
        memv_p.append(mv_p.reshape(batch, n_mem, N_MEM_HEADS, HEAD_DIM))
        mk_p = mk_p.reshape(batch, n_mem, MEM_WIDTH)
        mv_p = mv_p.reshape(batch, n_mem, MEM_WIDTH)
        mk_s = cache_mem_k[l].reshape(dec_batch, n_mem, MEM_WIDTH)
        mv_s = cache_mem_v[l].reshape(dec_batch, n_mem, MEM_WIDTH)
        xs8 = _pad_rows(xs)
        if l % 2 == 0:
            w = w_in_sb[i]
            ws = [w[:, :SB_WIDTH], w[:, SB_WIDTH:2 * SB_WIDTH], w[:, 2 * SB_WIDTH:3 * SB_WIDTH], w[:, 3 * SB_WIDTH:]]
            q, k, v, qm = norm_matmul(xp, norm_mix_g[l], ws, 512)
            sbk_p.append(k.reshape(batch, seq, N_SB_HEADS, HEAD_DIM))
            sbv_p.append(v.reshape(batch, seq, N_SB_HEADS, HEAD_DIM))
            o_p = sb_attention_prompt(q, k, v, sb_bias[i], batch, seq)
            qs, ks, vs, qms = norm_matmul(xs8, norm_mix_g[l], ws, 256)
            sbk_s.append(_unpad_rows(ks).reshape(dec_batch, 1, N_SB_HEADS, HEAD_DIM))
            sbv_s.append(_unpad_rows(vs).reshape(dec_batch, 1, N_SB_HEADS, HEAD_DIM))
            o_s = sb_attention_sample(_unpad_rows(qs), cache_sb_k[i].reshape(n_phys, PAGE_SIZE, SB_WIDTH),
                                      cache_sb_v[i].reshape(n_phys, PAGE_SIZE, SB_WIDTH), page_table, sb_bias[i])
            o_s = _pad_rows(o_s)
        else:
            w = w_in_gm[i]
            ws = [w[:, :GM_WIDTH], w[:, GM_WIDTH:2 * GM_WIDTH], w[:, 2 * GM_WIDTH:]]
            ws_full = jnp.zeros((GM_GROUPS, CHUNK, CHUNK), F32).at[:, rows, cols].set(gm_ws_tril[i])
            pu, pv, qm = norm_matmul(xp, norm_mix_g[l], ws, 512)
            o_p, _ = gm_gate(pu, pv, gm_norm_g[i], ws_full, gm_b[i], CHUNK)
            pus, pvs, qms = norm_matmul(xs8, norm_mix_g[l], ws, 256)
            o_s, v_s = gm_gate(pus, pvs, gm_norm_g[i], ws_full[:, :SAMPLE_ROWS, :SAMPLE_ROWS],
                               gm_b[i][:, :SAMPLE_ROWS], SAMPLE_ROWS)
            gmv_s.append(_unpad_rows(v_s).reshape(dec_batch, 1, GM_WIDTH))
        xp = mixer_out(xp, o_p, qm, mk_p, mv_p, w_out[l], seq, 512)
        xs = _unpad_rows(mixer_out(xs8, o_s, qms, mk_s, mv_s, w_out[l], SAMPLE_ROWS, SAMPLE_ROWS))
        moe_w = (norm_ffn_g[l], w_router_grp[l], b_router_grp[l], w_router_exp[l], b_router_exp[l],
                 w_gate[l], w_up[l], w_down[l])
        xp = hier_moe_residual(xp, *moe_w, 512, 256)
        xs = hier_moe_residual(xs, *moe_w, 32, 16)

    y_prompt = rmsnorm_rows(xp, final_norm_g, 1024).reshape(batch, seq, d)
    y_sample = rmsnorm_rows(xs, final_norm_g, 32).reshape(dec_batch, 1, d)
    return (y_prompt, y_sample, jnp.stack(sbk_p), jnp.stack(sbv_p), jnp.stack(sbk_s), jnp.stack(sbv_s),
            jnp.stack(memk_p), jnp.stack(memv_p), jnp.stack(gmv_s))
```

```python
import functools

import numpy as np
import jax
import jax.numpy as jnp
from jax import lax
from jax.experimental import pallas as pl
from jax.experimental.pallas import tpu as pltpu

F32 = jnp.float32
BF16 = jnp.bfloat16
I32 = jnp.int32

HEAD_DIM = 64
LANES = 128
SUBLANES = 8
N_SB_HEADS = 12
SB_WIDTH = N_SB_HEADS * HEAD_DIM
N_MEM_HEADS = 4
MEM_WIDTH = N_MEM_HEADS * HEAD_DIM
GM_GROUPS = 4
GM_WIDTH = 768
GM_GROUP_DIM = GM_WIDTH // GM_GROUPS
CHUNK = 128
PAGE_SIZE = 128
N_GROUPS = 4
EXPERTS_PER_GROUP = 8
N_EXPERTS = N_GROUPS * EXPERTS_PER_GROUP
TOP_K = 2
RMS_EPS = 1e-6
ATTN_SCALE = HEAD_DIM ** -0.5

SB_TK = 128
SB_TQ = 256
PAGES_PER_STEP = 8
SAMPLE_ROWS = 2 * SUBLANES
VMEM_LIMIT = 48 * 1024 * 1024
NT_DIMS = (((1,), (1,)), ((), ()))
HI16 = -65536


def _cparams(n_axes):
    return pltpu.CompilerParams(dimension_semantics=("arbitrary",) * n_axes,
                                vmem_limit_bytes=VMEM_LIMIT)


def _mm(a, b):
    return jnp.dot(a.astype(BF16), b.astype(BF16), preferred_element_type=F32)


def _mm_nt(a, b):
    return lax.dot_general(a.astype(BF16), b.astype(BF16), NT_DIMS, preferred_element_type=F32)


def _split_bf16(x):
    hi = x.astype(BF16)
    lo = (x - hi.astype(F32)).astype(BF16)
    return hi, lo


def _rms(x, g):
    return x * lax.rsqrt(jnp.mean(x * x, axis=-1, keepdims=True) + RMS_EPS) * g


def _gelu(x):
    return x * (0.5 * (1.0 + jnp.tanh(np.sqrt(2.0 / np.pi).astype(np.float32) * (x + 0.044715 * (x * x * x)))))


def _log_sigmoid_pair(z):
    l = jnp.log(1.0 + jnp.exp(-jnp.abs(z)))
    return jnp.minimum(z, 0.0) - l, -(jnp.maximum(z, 0.0) + l)


def _cumsum_weights():
    j = np.arange(SB_TK)[:, None]
    s = np.arange(SB_TK)[None, :]
    w = np.concatenate([(j > s).astype(np.float32), np.ones((SB_TK, SB_TK), np.float32)], axis=1)
    return jnp.asarray(np.concatenate([w, w], axis=0), dtype=BF16)


def _later_sums(log_keep, w2):
    hi, lo = _split_bf16(log_keep)
    return jnp.dot(jnp.concatenate([hi, lo], axis=1), w2, preferred_element_type=F32)


def _norm_matmul_kernel(x_ref, g_ref, *refs, n_out):
    h = _rms(x_ref[...], g_ref[...]).astype(BF16)
    for w_ref, o_ref in zip(refs[:n_out], refs[n_out:]):
        o_ref[...] = jnp.dot(h, w_ref[...], preferred_element_type=F32)


def norm_matmul(x, g, ws, tm):
    n, d = x.shape
    tm = min(tm, n)
    assert n % tm == 0
    ws = [w.astype(BF16) for w in ws]
    in_specs = [pl.BlockSpec((tm, d), lambda i: (i, 0)), pl.BlockSpec((1, d), lambda i: (0, 0))]
    in_specs += [pl.BlockSpec(w.shape, lambda i: (0, 0)) for w in ws]
    out_specs = [pl.BlockSpec((tm, w.shape[1]), lambda i: (i, 0)) for w in ws]
    out_shape = [jax.ShapeDtypeStruct((n, w.shape[1]), F32) for w in ws]
    return pl.pallas_call(
        functools.partial(_norm_matmul_kernel, n_out=len(ws)),
        grid=(n // tm,), in_specs=in_specs, out_specs=out_specs, out_shape=out_shape,
        compiler_params=_cparams(1))(x, g.reshape(1, d), *ws)


def _rmsnorm_kernel(x_ref, g_ref, o_ref):
    o_ref[...] = _rms(x_ref[...], g_ref[...])


def rmsnorm_rows(x, g, tm):
    n, d = x.shape
    tm = min(tm, n)
    assert n % tm == 0
    return pl.pallas_call(
        _rmsnorm_kernel, grid=(n // tm,),
        in_specs=[pl.BlockSpec((tm, d), lambda i: (i, 0)), pl.BlockSpec((1, d), lambda i: (0, 0))],
        out_specs=pl.BlockSpec((tm, d), lambda i: (i, 0)),
        out_shape=jax.ShapeDtypeStruct((n, d), F32), compiler_params=_cparams(1))(x, g.reshape(1, d))


def _sb_prompt_kernel(bias_ref, q_ref, k_ref, v_ref, w2_ref, o_ref, kb_ref, vb_ref, acc_ref, later_ref):
    hp = pl.program_id(1)
    qi = pl.program_id(2)

    @pl.when(qi == 0)
    def _():
        kb_ref[...] = k_ref[...].astype(BF16)
        vb_ref[...] = v_ref[...].astype(BF16)

    lane = lax.broadcasted_iota(I32, (SB_TQ, LANES), 1)
    head_masks = (lane < HEAD_DIM, lane >= HEAD_DIM)
    q = q_ref[...] * ATTN_SCALE
    qh = [jnp.where(m, q, 0.0).astype(BF16) for m in head_masks]
    bias = [bias_ref[2 * hp], bias_ref[2 * hp + 1]]
    w2 = w2_ref[...]
    row = lax.broadcasted_iota(I32, (SB_TQ, SB_TQ), 0)
    col = lax.broadcasted_iota(I32, (SB_TQ, SB_TQ), 1)
    strictly_before = col < row
    acc_ref[...] = jnp.zeros_like(acc_ref)
    later_ref[...] = jnp.zeros_like(later_ref)

    def key_step(j, diagonal):
        start = pl.multiple_of(j * SB_TQ, SB_TQ)
        kj = kb_ref[pl.ds(start, SB_TQ), :]
        vj = vb_ref[pl.ds(start, SB_TQ), :]
        for h in range(2):
            z = lax.dot_general(qh[h], kj, NT_DIMS, preferred_element_type=F32) + bias[h]
            log_beta, log_keep = _log_sigmoid_pair(z)
            if diagonal:
                log_keep = jnp.where(strictly_before, log_keep, 0.0)
            later = later_ref[h]
            parts = [None] * (SB_TQ // SB_TK)
            for s in reversed(range(SB_TQ // SB_TK)):
                cols = slice(s * SB_TK, (s + 1) * SB_TK)
                sums = _later_sums(log_keep[:, cols], w2)
                parts[s] = jnp.exp(log_beta[:, cols] + sums[:, :SB_TK] + later)
                later = later + sums[:, SB_TK:]
            a = jnp.concatenate(parts, axis=1)
            if diagonal:
                a = jnp.where(strictly_before, a, 0.0)
            acc_ref[h] += jnp.dot(a.astype(BF16), vj, preferred_element_type=F32)
            later_ref[h] = later

    key_step(qi, True)

    def body(t, c):
        key_step(qi - 1 - t, False)
        return c
    lax.fori_loop(0, qi, body, 0)
    o_ref[...] = jnp.where(head_masks[0], acc_ref[0], acc_ref[1])


def sb_attention_prompt(q, k, v, bias, batch, seq):
    n = q.shape[0]
    assert seq % SB_TQ == 0 and SB_TQ % SB_TK == 0 and n == batch * seq
    nq = seq // SB_TQ
    return pl.pallas_call(
        _sb_prompt_kernel,
        grid=(batch, N_SB_HEADS // 2, nq),
        in_specs=[
            pl.BlockSpec(memory_space=pltpu.SMEM),
            pl.BlockSpec((SB_TQ, LANES), lambda b, hp, qi: (b * nq + qi, hp)),
            pl.BlockSpec((seq, LANES), lambda b, hp, qi: (b, hp)),
            pl.BlockSpec((seq, LANES), lambda b, hp, qi: (b, hp)),
            pl.BlockSpec((2 * SB_TK, 2 * SB_TK), lambda b, hp, qi: (0, 0)),
        ],
        out_specs=pl.BlockSpec((SB_TQ, LANES), lambda b, hp, qi: (b * nq + qi, hp)),
        out_shape=jax.ShapeDtypeStruct((n, SB_WIDTH), F32),
        scratch_shapes=[pltpu.VMEM((seq, LANES), BF16), pltpu.VMEM((seq, LANES), BF16),
                        pltpu.VMEM((2, SB_TQ, LANES), F32), pltpu.VMEM((2, SB_TQ, LANES), F32)],
        compiler_params=_cparams(3))(bias, q, k, v, _cumsum_weights())


SB_HEAD_ROWS = 16


def _sb_sample_kernel(pt_ref, q_ref, bias_ref, w2_ref, *refs):
    del pt_ref
    k_refs = refs[:PAGES_PER_STEP]
    v_refs = refs[PAGES_PER_STEP:2 * PAGES_PER_STEP]
    o_ref = refs[2 * PAGES_PER_STEP]
    acc_ref, later_ref = refs[2 * PAGES_PER_STEP + 1:]
    j = pl.program_id(1)

    @pl.when(j == 0)
    def _():
        acc_ref[...] = jnp.zeros_like(acc_ref)
        later_ref[...] = jnp.zeros_like(later_ref)

    head = lax.broadcasted_iota(I32, (SB_HEAD_ROWS, SB_WIDTH), 0)
    chan_head = lax.broadcasted_iota(I32, (SB_HEAD_ROWS, SB_WIDTH), 1) // HEAD_DIM
    own = head == chan_head
    q_rows = jnp.where(own, q_ref[0] * ATTN_SCALE, 0.0).astype(BF16)
    bias = bias_ref[...]
    w2 = w2_ref[...]

    acc = acc_ref[...]
    later = later_ref[...]
    for i in range(PAGES_PER_STEP):
        z = jnp.dot(q_rows, k_refs[i][0].astype(BF16), preferred_element_type=F32) + bias
        log_beta, log_keep = _log_sigmoid_pair(z)
        sums = _later_sums(log_keep, w2)
        a = jnp.exp(log_beta + sums[:, :PAGE_SIZE] + later)
        acc = acc + lax.dot_general(a.astype(BF16), v_refs[i][0].astype(BF16), NT_DIMS,
                                    preferred_element_type=F32)
        later = later + sums[:, PAGE_SIZE:]
    acc_ref[...] = acc
    later_ref[...] = later

    @pl.when(j == pl.num_programs(1) - 1)
    def _():
        o_ref[0] = jnp.sum(jnp.where(own, acc, 0.0), axis=0, keepdims=True)


def sb_attention_sample(q, cache_kt, cache_vt, page_table, bias):
    nb, n_pages = page_table.shape
    assert n_pages % PAGES_PER_STEP == 0 and PAGE_SIZE == SB_TK
    steps = n_pages // PAGES_PER_STEP
    bias_rows = jnp.zeros((SB_HEAD_ROWS, PAGE_SIZE), F32).at[:N_SB_HEADS].set(
        jnp.broadcast_to(bias[:, None], (N_SB_HEADS, PAGE_SIZE)))

    def page_spec(i):
        return pl.BlockSpec((1, SB_WIDTH, PAGE_SIZE),
                            lambda b, j, pt: (pt[b, n_pages - 1 - (j * PAGES_PER_STEP + i)], 0, 0))

    grid_spec = pltpu.PrefetchScalarGridSpec(
        num_scalar_prefetch=1, grid=(nb, steps),
        in_specs=[pl.BlockSpec((1, 1, SB_WIDTH), lambda b, j, pt: (b, 0, 0)),
                  pl.BlockSpec((SB_HEAD_ROWS, PAGE_SIZE), lambda b, j, pt: (0, 0)),
                  pl.BlockSpec((2 * SB_TK, 2 * SB_TK), lambda b, j, pt: (0, 0))]
        + [page_spec(i) for i in range(PAGES_PER_STEP)] * 2,
        out_specs=pl.BlockSpec((1, 1, SB_WIDTH), lambda b, j, pt: (b, 0, 0)),
        scratch_shapes=[pltpu.VMEM((SB_HEAD_ROWS, SB_WIDTH), F32), pltpu.VMEM((SB_HEAD_ROWS, PAGE_SIZE), F32)])
    out = pl.pallas_call(
        _sb_sample_kernel, grid_spec=grid_spec,
        out_shape=jax.ShapeDtypeStruct((nb, 1, SB_WIDTH), F32),
        compiler_params=_cparams(2))(
            page_table, q.reshape(nb, 1, SB_WIDTH), bias_rows, _cumsum_weights(),
            *([cache_kt] * PAGES_PER_STEP), *([cache_vt] * PAGES_PER_STEP))
    return out.reshape(nb, SB_WIDTH)


def _pages_channel_major(cache):
    n_phys = cache.shape[0]
    return jnp.transpose(cache, (0, 2, 3, 1)).reshape(n_phys, SB_WIDTH, PAGE_SIZE)


def _gm_kernel(pu_ref, pv_ref, vg_ref, ws_ref, bias_ref, s_ref, v_ref):
    u = _gelu(pu_ref[...])
    v = _rms(_gelu(pv_ref[...]), vg_ref[...])
    v_ref[...] = v
    group = lax.broadcasted_iota(I32, v.shape, 1) // GM_GROUP_DIM
    vb = v.astype(BF16)
    mixed = bias_ref[...]
    for g in range(GM_GROUPS):
        mixed = mixed + jnp.where(group == g, jnp.dot(ws_ref[g], vb, preferred_element_type=F32), 0.0)
    s_ref[...] = u * mixed


def gm_gate(pu, pv, vg, ws_full, bias, chunk):
    n = pu.shape[0]
    assert n % chunk == 0
    bias_full = jnp.repeat(bias.T, GM_GROUP_DIM, axis=1)
    row = lambda i: (i, 0)
    return pl.pallas_call(
        _gm_kernel, grid=(n // chunk,),
        in_specs=[pl.BlockSpec((chunk, GM_WIDTH), row), pl.BlockSpec((chunk, GM_WIDTH), row),
                  pl.BlockSpec((1, GM_WIDTH), lambda i: (0, 0)),
                  pl.BlockSpec((GM_GROUPS, chunk, chunk), lambda i: (0, 0, 0)),
                  pl.BlockSpec((chunk, GM_WIDTH), lambda i: (0, 0))],
        out_specs=[pl.BlockSpec((chunk, GM_WIDTH), row), pl.BlockSpec((chunk, GM_WIDTH), row)],
        out_shape=[jax.ShapeDtypeStruct((n, GM_WIDTH), F32)] * 2,
        compiler_params=_cparams(1))(pu, pv, vg.reshape(1, GM_WIDTH), ws_full.astype(BF16), bias_full)


def _mixer_out_kernel(x_ref, o_ref, qm_ref, mk_ref, mv_ref, wo_ref, out_ref):
    qm = qm_ref[...] * ATTN_SCALE
    mk = mk_ref[0].astype(BF16)
    mv = mv_ref[0].astype(BF16)
    head = lax.broadcasted_iota(I32, qm.shape, 1) // HEAD_DIM
    o_mem = jnp.zeros(qm.shape, F32)
    for h in range(N_MEM_HEADS):
        own = head == h
        s = _mm_nt(jnp.where(own, qm, 0.0), mk)
        p = jnp.exp(s - jnp.max(s, axis=-1, keepdims=True))
        p = p / jnp.sum(p, axis=-1, keepdims=True)
        o_mem = jnp.where(own, _mm(p, mv), o_mem)
    wo = wo_ref[...]
    width = o_ref.shape[1]
    out_ref[...] = x_ref[...] + _mm(o_ref[...], wo[:width]) + _mm(o_mem, wo[width:])


def mixer_out(x, o_mix, qm, mk, mv, wo, rows_per_batch, tm):
    n, d = x.shape
    tm = min(tm, rows_per_batch)
    assert rows_per_batch % tm == 0 and n % tm == 0
    per = rows_per_batch // tm
    wo = wo.astype(BF16)
    row = lambda i: (i, 0)
    mem = lambda i: (i // per, 0, 0)
    return pl.pallas_call(
        _mixer_out_kernel, grid=(n // tm,),
        in_specs=[pl.BlockSpec((tm, d), row), pl.BlockSpec((tm, o_mix.shape[1]), row),
                  pl.BlockSpec((tm, MEM_WIDTH), row),
                  pl.BlockSpec((1,) + mk.shape[1:], mem), pl.BlockSpec((1,) + mv.shape[1:], mem),
                  pl.BlockSpec(wo.shape, lambda i: (0, 0))],
        out_specs=pl.BlockSpec((tm, d), row),
        out_shape=jax.ShapeDtypeStruct((n, d), F32),
        compiler_params=_cparams(1))(x, o_mix, qm, mk, mv, wo)


ROUTE_LANES = LANES
REC = SUBLANES
REC_W1, REC_W2, REC_E1, REC_E2 = 4, 5, 6, 7


def _strided(ref, first, count):
    return ref.at[pl.ds(first, count, stride=REC), :]


def _route_kernel(x_ref, g_ref, w_ref, b_ref, rec_ref, re_ref):
    t = _rms(x_ref[...], g_ref[...])
    tm, d = t.shape
    logits = _mm(t, w_ref[...]) + b_ref[...]
    lane = lax.broadcasted_iota(I32, logits.shape, 1).astype(F32)
    neg = jnp.float32(-jnp.inf)
    far = jnp.float32(ROUTE_LANES)

    def first_max(vals):
        m = jnp.max(vals, axis=-1, keepdims=True)
        return m, jnp.min(jnp.where(vals == m, lane, far), axis=-1, keepdims=True)

    is_group = lane < N_GROUPS
    gmax, gsel = first_max(jnp.where(is_group, logits, neg))
    p_grp = 1.0 / jnp.sum(jnp.where(is_group, jnp.exp(logits - gmax), 0.0), axis=-1, keepdims=True)
    lo = N_GROUPS + gsel * EXPERTS_PER_GROUP
    cand = jnp.where((lane >= lo) & (lane < lo + EXPERTS_PER_GROUP), logits, neg)
    m1, i1 = first_max(cand)
    m2, i2 = first_max(jnp.where(lane == i1, neg, cand))
    e2 = jnp.exp(m2 - m1)
    den = 1.0 + e2
    w1 = (1.0 / den) * p_grp
    w2 = (e2 / den) * p_grp
    ex1 = (i1 - N_GROUPS).astype(I32)
    ex2 = (i2 - N_GROUPS).astype(I32)
    re_ref[...] = jnp.where(lane == 0, ex1, jnp.where(lane == 1, ex2, 0))

    bits = lax.bitcast_convert_type(t.astype(BF16).astype(F32), I32)
    half = d // 2
    words = bits[:, half:] | lax.shift_right_logical(bits[:, :half], 16)
    for j in range(half // LANES):
        _strided(rec_ref, j, tm)[...] = words[:, j * LANES:(j + 1) * LANES]
    full = (tm, LANES)
    _strided(rec_ref, REC_W1, tm)[...] = lax.bitcast_convert_type(jnp.broadcast_to(w1, full), I32)
    _strided(rec_ref, REC_W2, tm)[...] = lax.bitcast_convert_type(jnp.broadcast_to(w2, full), I32)
    _strided(rec_ref, REC_E1, tm)[...] = jnp.broadcast_to(ex1, full)
    _strided(rec_ref, REC_E2, tm)[...] = jnp.broadcast_to(ex2, full)


def moe_route(x, g, wg, bg, wr, br, tm):
    n, d = x.shape
    tm = min(tm, n)
    assert n % tm == 0 and d == 2 * (REC_W1 * LANES)
    pad = ROUTE_LANES - N_GROUPS - N_EXPERTS
    w = jnp.concatenate([wg, wr, jnp.zeros((d, pad), F32)], axis=1).astype(BF16)
    b = jnp.concatenate([bg, br, jnp.zeros((pad,), F32)]).reshape(1, ROUTE_LANES)
    row = lambda i: (i, 0)
    return pl.pallas_call(
        _route_kernel, grid=(n // tm,),
        in_specs=[pl.BlockSpec((tm, d), row), pl.BlockSpec((1, d), lambda i: (0, 0)),
                  pl.BlockSpec((d, ROUTE_LANES), lambda i: (0, 0)), pl.BlockSpec((1, ROUTE_LANES), lambda i: (0, 0))],
        out_specs=[pl.BlockSpec((tm * REC, LANES), row), pl.BlockSpec((tm, ROUTE_LANES), row)],
        out_shape=[jax.ShapeDtypeStruct((n * REC, LANES), I32), jax.ShapeDtypeStruct((n, ROUTE_LANES), I32)],
        compiler_params=_cparams(1))(x, g.reshape(1, d), w, b)


def _record(ref, index):
    return ref.at[pl.ds(pl.multiple_of(index * REC, REC), REC)]


def _tile_index_specs(nt, width):
    cur = pl.BlockSpec((1, 1, width), lambda i: (i, 0, 0), memory_space=pltpu.SMEM)
    nxt = pl.BlockSpec((1, 1, width), lambda i: (jnp.minimum(i + 1, nt - 1), 0, 0), memory_space=pltpu.SMEM)
    return [cur, nxt]


def _dispatch_kernel(pos_ref, src_hbm, init_hbm, dst_hbm, sem, *, tokens):
    del init_hbm
    i = pl.program_id(0)
    slot = i % 2
    base = i * tokens

    def body(r, c):
        src = _record(src_hbm, base + r)
        for k in range(TOP_K):
            pltpu.make_async_copy(src, _record(dst_hbm, pos_ref[0, 0, k * tokens + r]), sem.at[slot]).start()
        return c
    lax.fori_loop(0, tokens, body, 0)

    def wait_step(s):
        n = TOP_K * tokens * REC
        pltpu.make_async_copy(src_hbm.at[pl.ds(0, n)], dst_hbm.at[pl.ds(0, n)], sem.at[s]).wait()

    @pl.when(i > 0)
    def _():
        wait_step(1 - slot)

    @pl.when(i == pl.num_programs(0) - 1)
    def _():
        wait_step(slot)


def dispatch_records(rec, pos, total, tokens):
    n = pos.shape[0]
    tokens = min(tokens, n)
    assert n % tokens == 0 and n >= TOP_K * tokens
    nt = n // tokens
    idx3 = pos.reshape(nt, tokens, TOP_K).transpose(0, 2, 1).reshape(nt, 1, TOP_K * tokens)
    return pl.pallas_call(
        functools.partial(_dispatch_kernel, tokens=tokens), grid=(nt,),
        in_specs=[pl.BlockSpec((1, 1, TOP_K * tokens), lambda i: (i, 0, 0), memory_space=pltpu.SMEM),
                  pl.BlockSpec(memory_space=pl.ANY), pl.BlockSpec(memory_space=pl.ANY)],
        out_specs=pl.BlockSpec(memory_space=pl.ANY),
        out_shape=jax.ShapeDtypeStruct((total * REC, LANES), I32),
        scratch_shapes=[pltpu.SemaphoreType.DMA((2,))],
        input_output_aliases={2: 0},
        compiler_params=_cparams(1))(idx3, rec, jnp.zeros((total * REC, LANES), I32))


def _expert_kernel(te_ref, used_ref, rec_ref, wg_ref, wu_ref, wd_ref, y_ref, wg, wu, wd, *, tm):
    i = pl.program_id(0)
    prev = te_ref[jnp.maximum(i - 1, 0)]

    @pl.when((i == 0) | (te_ref[i] != prev))
    def _():
        wg[...] = wg_ref[0].astype(BF16)
        wu[...] = wu_ref[0].astype(BF16)
        wd[...] = wd_ref[0].astype(BF16)

    @pl.when(i < used_ref[0])
    def _():
        low, high = [], []
        for j in range(REC_W1):
            words = _strided(rec_ref, j, tm)[...]
            low.append(lax.bitcast_convert_type(lax.shift_left(words, 16), F32))
            high.append(lax.bitcast_convert_type(words & HI16, F32))
        t = jnp.concatenate(low + high, axis=1).astype(BF16)
        w1 = lax.bitcast_convert_type(_strided(rec_ref, REC_W1, tm)[...], F32)
        w2 = lax.bitcast_convert_type(_strided(rec_ref, REC_W2, tm)[...], F32)
        gate = jnp.where(_strided(rec_ref, REC_E1, tm)[...] == te_ref[i], w1, w2)
        a = jnp.dot(t, wg[...], preferred_element_type=F32)
        c = jnp.dot(t, wu[...], preferred_element_type=F32)
        gate = jnp.concatenate([gate] * (a.shape[1] // LANES), axis=1)
        hid = a * (1.0 / (1.0 + jnp.exp(-a))) * c * gate
        y = _mm(hid, wd[...])
        for j in range(y.shape[1] // LANES):
            _strided(y_ref, j, tm)[...] = y[:, j * LANES:(j + 1) * LANES]

    @pl.when(i >= used_ref[0])
    def _():
        y_ref[...] = jnp.zeros_like(y_ref)


def moe_experts(rec_sorted, tile_expert, n_used, w_gate, w_up, w_down, tm):
    p = rec_sorted.shape[0] // REC
    _, d, f = w_gate.shape
    assert p % tm == 0 and d == REC * LANES and f % LANES == 0
    grid_spec = pltpu.PrefetchScalarGridSpec(
        num_scalar_prefetch=2, grid=(p // tm,),
        in_specs=[pl.BlockSpec((tm * REC, LANES), lambda i, te, nu: (i, 0)),
                  pl.BlockSpec((1, d, f), lambda i, te, nu: (te[i], 0, 0)),
                  pl.BlockSpec((1, d, f), lambda i, te, nu: (te[i], 0, 0)),
                  pl.BlockSpec((1, f, d), lambda i, te, nu: (te[i], 0, 0))],
        out_specs=pl.BlockSpec((tm * REC, LANES), lambda i, te, nu: (i, 0)),
        scratch_shapes=[pltpu.VMEM((d, f), BF16), pltpu.VMEM((d, f), BF16), pltpu.VMEM((f, d), BF16)])
    return pl.pallas_call(
        functools.partial(_expert_kernel, tm=tm), grid_spec=grid_spec,
        out_shape=jax.ShapeDtypeStruct((p * REC, LANES), F32),
        compiler_params=_cparams(1))(tile_expert, n_used, rec_sorted, w_gate, w_up, w_down)


def _combine_kernel(idx_ref, nxt_ref, x_ref, y_hbm, out_ref, buf, sem, *, rows):
    i = pl.program_id(0)
    slot = i % 2
    count = TOP_K * rows

    def issue(ref, s):
        def body(r, c):
            pltpu.make_async_copy(_record(y_hbm, ref[0, 0, r]), _record(buf.at[s], r), sem.at[s]).start()
            return c
        lax.fori_loop(0, count, body, 0)

    @pl.when(i == 0)
    def _():
        issue(idx_ref, 0)

    @pl.when(i + 1 < pl.num_programs(0))
    def _():
        issue(nxt_ref, 1 - slot)

    pltpu.make_async_copy(y_hbm.at[pl.ds(0, count * REC)], buf.at[slot], sem.at[slot]).wait()
    for j in range(REC):
        cols = slice(j * LANES, (j + 1) * LANES)
        first = buf[slot, pl.ds(j, rows, stride=REC), :]
        second = buf[slot, pl.ds(rows * REC + j, rows, stride=REC), :]
        out_ref[:, cols] = x_ref[:, cols] + first + second


def moe_combine(x, y_rec, pos, rows):
    n, d = x.shape
    rows = min(rows, n)
    assert n % rows == 0 and y_rec.shape[0] >= TOP_K * rows * REC and d == REC * LANES
    nt = n // rows
    idx3 = pos.reshape(nt, rows, TOP_K).transpose(0, 2, 1).reshape(nt, 1, TOP_K * rows)
    return pl.pallas_call(
        functools.partial(_combine_kernel, rows=rows), grid=(nt,),
        in_specs=_tile_index_specs(nt, TOP_K * rows)
        + [pl.BlockSpec((rows, d), lambda i: (i, 0)), pl.BlockSpec(memory_space=pl.ANY)],
        out_specs=pl.BlockSpec((rows, d), lambda i: (i, 0)),
        out_shape=jax.ShapeDtypeStruct((n, d), F32),
        scratch_shapes=[pltpu.VMEM((2, TOP_K * rows * REC, LANES), F32), pltpu.SemaphoreType.DMA((2,))],
        compiler_params=_cparams(1))(idx3, idx3, x, y_rec)


PLAN_BLOCK = 256


def _moe_plan(route_e, tm):
    n = route_e.shape[0]
    e = route_e[:, :TOP_K].reshape(-1)
    na = e.shape[0]
    blk = min(PLAN_BLOCK, na)
    assert na % blk == 0
    onehot = (e[:, None] == jnp.arange(N_EXPERTS, dtype=I32)[None, :]).astype(I32).reshape(na // blk, blk, N_EXPERTS)
    within = jnp.cumsum(onehot, axis=1)
    block_tot = within[:, -1, :]
    before = jnp.cumsum(block_tot, axis=0) - block_tot
    counts = jnp.sum(block_tot, axis=0)
    padded = ((counts + tm - 1) // tm) * tm
    pad_end = jnp.cumsum(padded)
    pad_start = pad_end - padded
    slot_if = within - 1 + before[:, None, :] + pad_start[None, None, :]
    pos = jnp.sum(onehot * slot_if, axis=2).reshape(n, TOP_K)
    total = ((na + tm - 1) // tm) * tm + N_EXPERTS * tm
    tile_start = jnp.arange(total // tm, dtype=I32) * tm
    tile_expert = jnp.minimum(jnp.searchsorted(pad_end, tile_start, side="right"), N_EXPERTS - 1).astype(I32)
    n_used = (pad_end[-1:] // tm).astype(I32)
    return pos, tile_expert, n_used, total


def hier_moe_residual(x, g, wg, bg, wr, br, w_gate, w_up, w_down, tm_tokens, tm_expert):
    rec, route_e = moe_route(x, g, wg, bg, wr, br, tm_tokens)
    pos, tile_expert, n_used, total = _moe_plan(route_e, tm_expert)
    rec_sorted = dispatch_records(rec, pos, total, tm_expert)
    y_rec = moe_experts(rec_sorted, tile_expert, n_used, w_gate, w_up, w_down, tm_expert)
    return moe_combine(x, y_rec, pos, tm_expert)


def _pad_rows(x):
    b, c = x.shape
    return jnp.zeros((b, SAMPLE_ROWS, c), x.dtype).at[:, 0].set(x).reshape(b * SAMPLE_ROWS, c)


def _unpad_rows(x):
    return x.reshape(-1, SAMPLE_ROWS, x.shape[1])[:, 0]


def kernel(x_prompt, x_sample, mem_prompt, cache_sb_k, cache_sb_v, cache_mem_k, cache_mem_v, page_table, norm_mix_g, norm_ffn_g, norm_mem_g, w_mem_kv, w_in_sb, sb_bias, w_in_gm, gm_norm_g, gm_ws_tril, gm_b, w_out, w_router_grp, b_router_grp, w_router_exp, b_router_exp, w_gate, w_up, w_down, final_norm_g):
    batch, seq, d = x_prompt.shape
    dec_batch, dec_seq, _ = x_sample.shape
    assert dec_seq == 1
    n_mem = mem_prompt.shape[1]
    depth = w_out.shape[0]
    rows, cols = np.tril_indices(CHUNK)

    xp = x_prompt.reshape(batch * seq, d)
    xs = x_sample.reshape(dec_batch, d)
    memp = mem_prompt.reshape(batch * n_mem, d)
    sbk_p, sbv_p, sbk_s, sbv_s, memk_p, memv_p, gmv_s = [], [], [], [], [], [], []

    for l in range(depth):
        i = l // 2
        mk_p, mv_p = norm_matmul(memp, norm_mem_g[l], [w_mem_kv[l][:, :MEM_WIDTH], w_mem_kv[l][:, MEM_WIDTH:]], 512)
        memk_p.append(mk_p.reshape(batch, n_mem, N_MEM_HEADS, HEAD_DIM))
        memv_p.append(mv_p.reshape(batch, n_mem, N_MEM_HEADS, HEAD_DIM))
        mk_p = mk_p.reshape(batch, n_mem, MEM_WIDTH)
        mv_p = mv_p.reshape(batch, n_mem, MEM_WIDTH)
        mk_s = cache_mem_k[l].reshape(dec_batch, n_mem, MEM_WIDTH)
        mv_s = cache_mem_v[l].reshape(dec_batch, n_mem, MEM_WIDTH)
        xs8 = _pad_rows(xs)
        if l % 2 == 0:
            w = w_in_sb[i]
            ws = [w[:, :SB_WIDTH], w[:, SB_WIDTH:2 * SB_WIDTH], w[:, 2 * SB_WIDTH:3 * SB_WIDTH], w[:, 3 * SB_WIDTH:]]
            q, k, v, qm = norm_matmul(xp, norm_mix_g[l], ws, 512)
            sbk_p.append(k.reshape(batch, seq, N_SB_HEADS, HEAD_DIM))
            sbv_p.append(v.reshape(batch, seq, N_SB_HEADS, HEAD_DIM))
            o_p = sb_attention_prompt(q, k, v, sb_bias[i], batch, seq)
            qs, ks, vs, qms = norm_matmul(xs8, norm_mix_g[l], ws, 512)
            sbk_s.append(_unpad_rows(ks).reshape(dec_batch, 1, N_SB_HEADS, HEAD_DIM))
            sbv_s.append(_unpad_rows(vs).reshape(dec_batch, 1, N_SB_HEADS, HEAD_DIM))
            o_s = sb_attention_sample(_unpad_rows(qs), _pages_channel_major(cache_sb_k[i]),
                                      _pages_channel_major(cache_sb_v[i]), page_table, sb_bias[i])
            o_s = _pad_rows(o_s)
        else:
            w = w_in_gm[i]
            ws = [w[:, :GM_WIDTH], w[:, GM_WIDTH:2 * GM_WIDTH], w[:, 2 * GM_WIDTH:]]
            ws_full = jnp.zeros((GM_GROUPS, CHUNK, CHUNK), F32).at[:, rows, cols].set(gm_ws_tril[i])
            pu, pv, qm = norm_matmul(xp, norm_mix_g[l], ws, 512)
            o_p, _ = gm_gate(pu, pv, gm_norm_g[i], ws_full, gm_b[i], CHUNK)
            pus, pvs, qms = norm_matmul(xs8, norm_mix_g[l], ws, 512)
            o_s, v_s = gm_gate(pus, pvs, gm_norm_g[i], ws_full[:, :SAMPLE_ROWS, :SAMPLE_ROWS],
                               gm_b[i][:, :SAMPLE_ROWS], SAMPLE_ROWS)
            gmv_s.append(_unpad_rows(v_s).reshape(dec_batch, 1, GM_WIDTH))
        xp = mixer_out(xp, o_p, qm, mk_p, mv_p, w_out[l], seq, 512)
        xs = _unpad_rows(mixer_out(xs8, o_s, qms, mk_s, mv_s, w_out[l], SAMPLE_ROWS, SAMPLE_ROWS))
        moe_w = (norm_ffn_g[l], w_router_grp[l], b_router_grp[l], w_router_exp[l], b_router_exp[l],
                 w_gate[l], w_up[l], w_down[l])
        xp = hier_moe_residual(xp, *moe_w, 512, 256)
        xs = hier_moe_residual(xs, *moe_w, 32, 16)

    y_prompt = rmsnorm_rows(xp, final_norm_g, 1024).reshape(batch, seq, d)
    y_sample = rmsnorm_rows(xs, final_norm_g, 32).reshape(dec_batch, 1, d)
    return (y_prompt, y_sample, jnp.stack(sbk_p), jnp.stack(sbv_p), jnp.stack(sbk_s), jnp.stack(sbv_s),
            jnp.stack(memk_p), jnp.stack(memv_p), jnp.stack(gmv_s))
```

```python
import functools

import numpy as np
import jax
import jax.numpy as jnp
from jax import lax
from jax.experimental import pallas as pl
from jax.experimental.pallas import tpu as pltpu

F32 = jnp.float32
BF16 = jnp.bfloat16
I32 = jnp.int32

HEAD_DIM = 64
LANES = 128
SUBLANES = 8
N_SB_HEADS = 12
SB_WIDTH = N_SB_HEADS * HEAD_DIM
N_MEM_HEADS = 4
MEM_WIDTH = N_MEM_HEADS * HEAD_DIM
GM_GROUPS = 4
GM_WIDTH = 768
GM_GROUP_DIM = GM_WIDTH // GM_GROUPS
CHUNK = 128
PAGE_SIZE = 128
N_GROUPS = 4
EXPERTS_PER_GROUP = 8
N_EXPERTS = N_GROUPS * EXPERTS_PER_GROUP
TOP_K = 2
RMS_EPS = 1e-6
ATTN_SCALE = HEAD_DIM ** -0.5

SB_TK = 128
SB_TQ = 256
PAGES_PER_STEP = 8
SAMPLE_ROWS = 2 * SUBLANES
VMEM_LIMIT = 48 * 1024 * 1024
NT_DIMS = (((1,), (1,)), ((), ()))
HI16 = -65536
ISSUE_UNROLL = 8


def _cparams(n_axes):
    return pltpu.CompilerParams(dimension_semantics=("arbitrary",) * n_axes,
                                vmem_limit_bytes=VMEM_LIMIT)


def _mm(a, b):
    return jnp.dot(a.astype(BF16), b.astype(BF16), preferred_element_type=F32)


def _mm_nt(a, b):
    return lax.dot_general(a.astype(BF16), b.astype(BF16), NT_DIMS, preferred_element_type=F32)


def _split_bf16(x):
    hi = x.astype(BF16)
    lo = (x - hi.astype(F32)).astype(BF16)
    return hi, lo


def _round_bf16(x):
    return x.astype(BF16).astype(F32)


def _rms(x, g):
    return x * lax.rsqrt(jnp.mean(x * x, axis=-1, keepdims=True) + RMS_EPS) * g


def _gelu(x):
    return x * (0.5 * (1.0 + jnp.tanh(np.sqrt(2.0 / np.pi).astype(np.float32) * (x + 0.044715 * (x * x * x)))))


def _log_sigmoid_pair(z):
    l = jnp.log(1.0 + jnp.exp(-jnp.abs(z)))
    return jnp.minimum(z, 0.0) - l, -(jnp.maximum(z, 0.0) + l)


def _cumsum_weights():
    j = np.arange(SB_TK)[:, None]
    s = np.arange(SB_TK)[None, :]
    w = np.concatenate([(j > s).astype(np.float32), np.ones((SB_TK, SB_TK), np.float32)], axis=1)
    return jnp.asarray(np.concatenate([w, w], axis=0), dtype=BF16)


def _later_sums(log_keep, w2):
    hi, lo = _split_bf16(log_keep)
    return jnp.dot(jnp.concatenate([hi, lo], axis=1), w2, preferred_element_type=F32)


def _norm_matmul_kernel(x_ref, g_ref, *refs, n_out):
    h = _rms(x_ref[...], g_ref[...]).astype(BF16)
    for w_ref, o_ref in zip(refs[:n_out], refs[n_out:]):
        o_ref[...] = jnp.dot(h, w_ref[...], preferred_element_type=F32)


def norm_matmul(x, g, ws, tm):
    n, d = x.shape
    tm = min(tm, n)
    assert n % tm == 0
    ws = [w.astype(BF16) for w in ws]
    in_specs = [pl.BlockSpec((tm, d), lambda i: (i, 0)), pl.BlockSpec((1, d), lambda i: (0, 0))]
    in_specs += [pl.BlockSpec(w.shape, lambda i: (0, 0)) for w in ws]
    out_specs = [pl.BlockSpec((tm, w.shape[1]), lambda i: (i, 0)) for w in ws]
    out_shape = [jax.ShapeDtypeStruct((n, w.shape[1]), F32) for w in ws]
    return pl.pallas_call(
        functools.partial(_norm_matmul_kernel, n_out=len(ws)),
        grid=(n // tm,), in_specs=in_specs, out_specs=out_specs, out_shape=out_shape,
        compiler_params=_cparams(1))(x, g.reshape(1, d), *ws)


def _rmsnorm_kernel(x_ref, g_ref, o_ref):
    o_ref[...] = _rms(x_ref[...], g_ref[...])


def rmsnorm_rows(x, g, tm):
    n, d = x.shape
    tm = min(tm, n)
    assert n % tm == 0
    return pl.pallas_call(
        _rmsnorm_kernel, grid=(n // tm,),
        in_specs=[pl.BlockSpec((tm, d), lambda i: (i, 0)), pl.BlockSpec((1, d), lambda i: (0, 0))],
        out_specs=pl.BlockSpec((tm, d), lambda i: (i, 0)),
        out_shape=jax.ShapeDtypeStruct((n, d), F32), compiler_params=_cparams(1))(x, g.reshape(1, d))


def _sb_prompt_kernel(bias_ref, q_ref, k_ref, v_ref, w2_ref, o_ref, kb_ref, vb_ref, acc_ref, later_ref):
    hp = pl.program_id(1)
    qi = pl.program_id(2)

    @pl.when(qi == 0)
    def _():
        kb_ref[...] = k_ref[...].astype(BF16)
        vb_ref[...] = v_ref[...].astype(BF16)

    lane = lax.broadcasted_iota(I32, (SB_TQ, LANES), 1)
    head_masks = (lane < HEAD_DIM, lane >= HEAD_DIM)
    q = q_ref[...] * ATTN_SCALE
    qh = [jnp.where(m, q, 0.0).astype(BF16) for m in head_masks]
    bias = [bias_ref[2 * hp], bias_ref[2 * hp + 1]]
    w2 = w2_ref[...]
    row = lax.broadcasted_iota(I32, (SB_TQ, SB_TQ), 0)
    col = lax.broadcasted_iota(I32, (SB_TQ, SB_TQ), 1)
    strictly_before = col < row
    acc_ref[...] = jnp.zeros_like(acc_ref)
    later_ref[...] = jnp.zeros_like(later_ref)

    def key_step(j, diagonal):
        start = pl.multiple_of(j * SB_TQ, SB_TQ)
        kj = kb_ref[pl.ds(start, SB_TQ), :]
        vj = vb_ref[pl.ds(start, SB_TQ), :]
        for h in range(2):
            z = lax.dot_general(qh[h], kj, NT_DIMS, preferred_element_type=F32) + bias[h]
            log_beta, log_keep = _log_sigmoid_pair(z)
            if diagonal:
                log_keep = jnp.where(strictly_before, log_keep, 0.0)
            later = later_ref[h]
            parts = [None] * (SB_TQ // SB_TK)
            for s in reversed(range(SB_TQ // SB_TK)):
                cols = slice(s * SB_TK, (s + 1) * SB_TK)
                sums = _later_sums(log_keep[:, cols], w2)
                parts[s] = jnp.exp(log_beta[:, cols] + sums[:, :SB_TK] + later)
                later = later + sums[:, SB_TK:]
            a = jnp.concatenate(parts, axis=1)
            if diagonal:
                a = jnp.where(strictly_before, a, 0.0)
            acc_ref[h] += jnp.dot(a.astype(BF16), vj, preferred_element_type=F32)
            later_ref[h] = later

    key_step(qi, True)

    def body(t, c):
        key_step(qi - 1 - t, False)
        return c
    lax.fori_loop(0, qi, body, 0)
    o_ref[...] = jnp.where(head_masks[0], acc_ref[0], acc_ref[1])


def sb_attention_prompt(q, k, v, bias, batch, seq):
    n = q.shape[0]
    assert seq % SB_TQ == 0 and SB_TQ % SB_TK == 0 and n == batch * seq
    nq = seq // SB_TQ
    return pl.pallas_call(
        _sb_prompt_kernel,
        grid=(batch, N_SB_HEADS // 2, nq),
        in_specs=[
            pl.BlockSpec(memory_space=pltpu.SMEM),
            pl.BlockSpec((SB_TQ, LANES), lambda b, hp, qi: (b * nq + qi, hp)),
            pl.BlockSpec((seq, LANES), lambda b, hp, qi: (b, hp)),
            pl.BlockSpec((seq, LANES), lambda b, hp, qi: (b, hp)),
            pl.BlockSpec((2 * SB_TK, 2 * SB_TK), lambda b, hp, qi: (0, 0)),
        ],
        out_specs=pl.BlockSpec((SB_TQ, LANES), lambda b, hp, qi: (b * nq + qi, hp)),
        out_shape=jax.ShapeDtypeStruct((n, SB_WIDTH), F32),
        scratch_shapes=[pltpu.VMEM((seq, LANES), BF16), pltpu.VMEM((seq, LANES), BF16),
                        pltpu.VMEM((2, SB_TQ, LANES), F32), pltpu.VMEM((2, SB_TQ, LANES), F32)],
        compiler_params=_cparams(3))(bias, q, k, v, _cumsum_weights())


SB_HEAD_ROWS = 16


def _sb_sample_kernel(pt_ref, q_ref, bias_ref, w2_ref, *refs):
    del pt_ref
    k_refs = refs[:PAGES_PER_STEP]
    v_refs = refs[PAGES_PER_STEP:2 * PAGES_PER_STEP]
    o_ref = refs[2 * PAGES_PER_STEP]
    qcol_ref, z_ref, a_ref, acc_ref, later_ref = refs[2 * PAGES_PER_STEP + 1:]
    j = pl.program_id(1)

    @pl.when(j == 0)
    def _():
        acc_ref[...] = jnp.zeros_like(acc_ref)
        later_ref[...] = jnp.zeros_like(later_ref)
        z_ref[...] = jnp.zeros_like(z_ref)
        qcol_ref[...] = _round_bf16(jnp.broadcast_to(q_ref[0] * ATTN_SCALE, qcol_ref.shape))

    heads = [slice(h * HEAD_DIM, (h + 1) * HEAD_DIM) for h in range(N_SB_HEADS)]
    for i in range(PAGES_PER_STEP):
        for h, rows in enumerate(heads):
            prod = _round_bf16(k_refs[i][0, rows, :]) * qcol_ref[rows, :]
            z_ref[pl.ds(i * SB_HEAD_ROWS + h, 1), :] = jnp.sum(prod, axis=0, keepdims=True)
    log_beta, log_keep = _log_sigmoid_pair(z_ref[...] + bias_ref[...])
    sums = _later_sums(log_keep, w2_ref[...])
    later = later_ref[...]
    for i in range(PAGES_PER_STEP):
        r = slice(i * SB_HEAD_ROWS, (i + 1) * SB_HEAD_ROWS)
        a_ref[r, :] = _round_bf16(jnp.exp(log_beta[r] + sums[r, :PAGE_SIZE] + later))
        later = later + sums[r, PAGE_SIZE:]
    later_ref[...] = later
    for h, rows in enumerate(heads):
        acc = acc_ref[rows, :]
        for i in range(PAGES_PER_STEP):
            acc = acc + _round_bf16(v_refs[i][0, rows, :]) * a_ref[pl.ds(i * SB_HEAD_ROWS + h, 1), :]
        acc_ref[rows, :] = acc

    @pl.when(j == pl.num_programs(1) - 1)
    def _():
        o_ref[0] = jnp.sum(acc_ref[...], axis=1, keepdims=True)


def sb_attention_sample(q, cache_kt, cache_vt, page_table, bias):
    nb, n_pages = page_table.shape
    assert n_pages % PAGES_PER_STEP == 0 and PAGE_SIZE == SB_TK
    steps = n_pages // PAGES_PER_STEP
    bias_rows = jnp.zeros((SB_HEAD_ROWS, PAGE_SIZE), F32).at[:N_SB_HEADS].set(
        jnp.broadcast_to(bias[:, None], (N_SB_HEADS, PAGE_SIZE)))
    bias_rows = jnp.tile(bias_rows, (PAGES_PER_STEP, 1))
    step_rows = PAGES_PER_STEP * SB_HEAD_ROWS

    def page_spec(i):
        return pl.BlockSpec((1, SB_WIDTH, PAGE_SIZE),
                            lambda b, j, pt: (pt[b, n_pages - 1 - (j * PAGES_PER_STEP + i)], 0, 0))

    grid_spec = pltpu.PrefetchScalarGridSpec(
        num_scalar_prefetch=1, grid=(nb, steps),
        in_specs=[pl.BlockSpec((1, SB_WIDTH, 1), lambda b, j, pt: (b, 0, 0)),
                  pl.BlockSpec((step_rows, PAGE_SIZE), lambda b, j, pt: (0, 0)),
                  pl.BlockSpec((2 * SB_TK, 2 * SB_TK), lambda b, j, pt: (0, 0))]
        + [page_spec(i) for i in range(PAGES_PER_STEP)] * 2,
        out_specs=pl.BlockSpec((1, SB_WIDTH, 1), lambda b, j, pt: (b, 0, 0)),
        scratch_shapes=[pltpu.VMEM((SB_WIDTH, PAGE_SIZE), F32),
                        pltpu.VMEM((step_rows, PAGE_SIZE), F32),
                        pltpu.VMEM((step_rows, PAGE_SIZE), F32),
                        pltpu.VMEM((SB_WIDTH, PAGE_SIZE), F32),
                        pltpu.VMEM((SB_HEAD_ROWS, PAGE_SIZE), F32)])
    out = pl.pallas_call(
        _sb_sample_kernel, grid_spec=grid_spec,
        out_shape=jax.ShapeDtypeStruct((nb, SB_WIDTH, 1), F32),
        compiler_params=_cparams(2))(
            page_table, q.reshape(nb, SB_WIDTH, 1), bias_rows, _cumsum_weights(),
            *([cache_kt] * PAGES_PER_STEP), *([cache_vt] * PAGES_PER_STEP))
    return out.reshape(nb, SB_WIDTH)


def _pages_channel_major(cache):
    n_phys = cache.shape[0]
    return jnp.transpose(cache, (0, 2, 3, 1)).reshape(n_phys, SB_WIDTH, PAGE_SIZE)


def _gm_kernel(pu_ref, pv_ref, vg_ref, ws_ref, bias_ref, s_ref, v_ref):
    u = _gelu(pu_ref[...])
    v = _rms(_gelu(pv_ref[...]), vg_ref[...])
    v_ref[...] = v
    group = lax.broadcasted_iota(I32, v.shape, 1) // GM_GROUP_DIM
    vb = v.astype(BF16)
    mixed = bias_ref[...]
    for g in range(GM_GROUPS):
        mixed = mixed + jnp.where(group == g, jnp.dot(ws_ref[g], vb, preferred_element_type=F32), 0.0)
    s_ref[...] = u * mixed


def gm_gate(pu, pv, vg, ws_full, bias, chunk):
    n = pu.shape[0]
    assert n % chunk == 0
    bias_full = jnp.repeat(bias.T, GM_GROUP_DIM, axis=1)
    row = lambda i: (i, 0)
    return pl.pallas_call(
        _gm_kernel, grid=(n // chunk,),
        in_specs=[pl.BlockSpec((chunk, GM_WIDTH), row), pl.BlockSpec((chunk, GM_WIDTH), row),
                  pl.BlockSpec((1, GM_WIDTH), lambda i: (0, 0)),
                  pl.BlockSpec((GM_GROUPS, chunk, chunk), lambda i: (0, 0, 0)),
                  pl.BlockSpec((chunk, GM_WIDTH), lambda i: (0, 0))],
        out_specs=[pl.BlockSpec((chunk, GM_WIDTH), row), pl.BlockSpec((chunk, GM_WIDTH), row)],
        out_shape=[jax.ShapeDtypeStruct((n, GM_WIDTH), F32)] * 2,
        compiler_params=_cparams(1))(pu, pv, vg.reshape(1, GM_WIDTH), ws_full.astype(BF16), bias_full)


def _mixer_out_kernel(x_ref, o_ref, qm_ref, mk_ref, mv_ref, wo_ref, out_ref):
    qm = qm_ref[...] * ATTN_SCALE
    mk = mk_ref[0].astype(BF16)
    mv = mv_ref[0].astype(BF16)
    head = lax.broadcasted_iota(I32, qm.shape, 1) // HEAD_DIM
    o_mem = jnp.zeros(qm.shape, F32)
    for h in range(N_MEM_HEADS):
        own = head == h
        s = _mm_nt(jnp.where(own, qm, 0.0), mk)
        p = jnp.exp(s - jnp.max(s, axis=-1, keepdims=True))
        p = p / jnp.sum(p, axis=-1, keepdims=True)
        o_mem = jnp.where(own, _mm(p, mv), o_mem)
    wo = wo_ref[...]
    width = o_ref.shape[1]
    out_ref[...] = x_ref[...] + _mm(o_ref[...], wo[:width]) + _mm(o_mem, wo[width:])


def mixer_out(x, o_mix, qm, mk, mv, wo, rows_per_batch, tm):
    n, d = x.shape
    tm = min(tm, rows_per_batch)
    assert rows_per_batch % tm == 0 and n % tm == 0
    per = rows_per_batch // tm
    wo = wo.astype(BF16)
    row = lambda i: (i, 0)
    mem = lambda i: (i // per, 0, 0)
    return pl.pallas_call(
        _mixer_out_kernel, grid=(n // tm,),
        in_specs=[pl.BlockSpec((tm, d), row), pl.BlockSpec((tm, o_mix.shape[1]), row),
                  pl.BlockSpec((tm, MEM_WIDTH), row),
                  pl.BlockSpec((1,) + mk.shape[1:], mem), pl.BlockSpec((1,) + mv.shape[1:], mem),
                  pl.BlockSpec(wo.shape, lambda i: (0, 0))],
        out_specs=pl.BlockSpec((tm, d), row),
        out_shape=jax.ShapeDtypeStruct((n, d), F32),
        compiler_params=_cparams(1))(x, o_mix, qm, mk, mv, wo)


ROUTE_LANES = LANES
REC = SUBLANES
REC_W1, REC_W2, REC_E1, REC_E2 = 4, 5, 6, 7


def _strided(ref, first, count):
    return ref.at[pl.ds(first, count, stride=REC), :]


def _route_kernel(x_ref, g_ref, w_ref, b_ref, rec_ref, re_ref):
    t = _rms(x_ref[...], g_ref[...])
    tm, d = t.shape
    logits = _mm(t, w_ref[...]) + b_ref[...]
    lane = lax.broadcasted_iota(I32, logits.shape, 1).astype(F32)
    neg = jnp.float32(-jnp.inf)
    far = jnp.float32(ROUTE_LANES)

    def first_max(vals):
        m = jnp.max(vals, axis=-1, keepdims=True)
        return m, jnp.min(jnp.where(vals == m, lane, far), axis=-1, keepdims=True)

    is_group = lane < N_GROUPS
    gmax, gsel = first_max(jnp.where(is_group, logits, neg))
    p_grp = 1.0 / jnp.sum(jnp.where(is_group, jnp.exp(logits - gmax), 0.0), axis=-1, keepdims=True)
    lo = N_GROUPS + gsel * EXPERTS_PER_GROUP
    cand = jnp.where((lane >= lo) & (lane < lo + EXPERTS_PER_GROUP), logits, neg)
    m1, i1 = first_max(cand)
    m2, i2 = first_max(jnp.where(lane == i1, neg, cand))
    e2 = jnp.exp(m2 - m1)
    den = 1.0 + e2
    w1 = (1.0 / den) * p_grp
    w2 = (e2 / den) * p_grp
    ex1 = (i1 - N_GROUPS).astype(I32)
    ex2 = (i2 - N_GROUPS).astype(I32)
    re_ref[...] = jnp.where(lane == 0, ex1, jnp.where(lane == 1, ex2, 0))

    bits = lax.bitcast_convert_type(t.astype(BF16).astype(F32), I32)
    half = d // 2
    words = bits[:, half:] | lax.shift_right_logical(bits[:, :half], 16)
    for j in range(half // LANES):
        _strided(rec_ref, j, tm)[...] = words[:, j * LANES:(j + 1) * LANES]
    full = (tm, LANES)
    _strided(rec_ref, REC_W1, tm)[...] = lax.bitcast_convert_type(jnp.broadcast_to(w1, full), I32)
    _strided(rec_ref, REC_W2, tm)[...] = lax.bitcast_convert_type(jnp.broadcast_to(w2, full), I32)
    _strided(rec_ref, REC_E1, tm)[...] = jnp.broadcast_to(ex1, full)
    _strided(rec_ref, REC_E2, tm)[...] = jnp.broadcast_to(ex2, full)


def moe_route(x, g, wg, bg, wr, br, tm):
    n, d = x.shape
    tm = min(tm, n)
    assert n % tm == 0 and d == 2 * (REC_W1 * LANES)
    pad = ROUTE_LANES - N_GROUPS - N_EXPERTS
    w = jnp.concatenate([wg, wr, jnp.zeros((d, pad), F32)], axis=1).astype(BF16)
    b = jnp.concatenate([bg, br, jnp.zeros((pad,), F32)]).reshape(1, ROUTE_LANES)
    row = lambda i: (i, 0)
    return pl.pallas_call(
        _route_kernel, grid=(n // tm,),
        in_specs=[pl.BlockSpec((tm, d), row), pl.BlockSpec((1, d), lambda i: (0, 0)),
                  pl.BlockSpec((d, ROUTE_LANES), lambda i: (0, 0)), pl.BlockSpec((1, ROUTE_LANES), lambda i: (0, 0))],
        out_specs=[pl.BlockSpec((tm * REC, LANES), row), pl.BlockSpec((tm, ROUTE_LANES), row)],
        out_shape=[jax.ShapeDtypeStruct((n * REC, LANES), I32), jax.ShapeDtypeStruct((n, ROUTE_LANES), I32)],
        compiler_params=_cparams(1))(x, g.reshape(1, d), w, b)


def _record(ref, index):
    return ref.at[pl.ds(pl.multiple_of(index * REC, REC), REC)]


def _tile_index_specs(nt, width):
    cur = pl.BlockSpec((1, 1, width), lambda i: (i, 0, 0), memory_space=pltpu.SMEM)
    nxt = pl.BlockSpec((1, 1, width), lambda i: (jnp.minimum(i + 1, nt - 1), 0, 0), memory_space=pltpu.SMEM)
    return [cur, nxt]


def _dispatch_kernel(pos_ref, rec_ref, init_hbm, dst_hbm, stage, sem, *, tokens):
    del init_hbm
    i = pl.program_id(0)
    last = pl.num_programs(0) - 1
    slot = i % 2

    def wait_step(s):
        for _ in range(TOP_K):
            pltpu.make_async_copy(stage.at[s], dst_hbm.at[pl.ds(0, tokens * REC)], sem.at[s]).wait()

    @pl.when(i >= 2)
    def _():
        wait_step(slot)

    stage[slot] = rec_ref[...]

    def body(r, c):
        src = _record(stage.at[slot], r)
        for k in range(TOP_K):
            pltpu.make_async_copy(src, _record(dst_hbm, pos_ref[0, 0, k * tokens + r]), sem.at[slot]).start()
        return c
    lax.fori_loop(0, tokens, body, 0, unroll=ISSUE_UNROLL)

    @pl.when(i == last)
    def _():
        wait_step(slot)

    @pl.when((i == last) & (i >= 1))
    def _():
        wait_step(1 - slot)


def dispatch_records(rec, pos, total, tokens):
    n = pos.shape[0]
    tokens = min(tokens, n)
    assert n % tokens == 0 and total >= tokens
    nt = n // tokens
    idx3 = pos.reshape(nt, tokens, TOP_K).transpose(0, 2, 1).reshape(nt, 1, TOP_K * tokens)
    return pl.pallas_call(
        functools.partial(_dispatch_kernel, tokens=tokens), grid=(nt,),
        in_specs=[pl.BlockSpec((1, 1, TOP_K * tokens), lambda i: (i, 0, 0), memory_space=pltpu.SMEM),
                  pl.BlockSpec((tokens * REC, LANES), lambda i: (i, 0)), pl.BlockSpec(memory_space=pl.ANY)],
        out_specs=pl.BlockSpec(memory_space=pl.ANY),
        out_shape=jax.ShapeDtypeStruct((total * REC, LANES), I32),
        scratch_shapes=[pltpu.VMEM((2, tokens * REC, LANES), I32), pltpu.SemaphoreType.DMA((2,))],
        input_output_aliases={2: 0},
        compiler_params=_cparams(1))(idx3, rec, jnp.zeros((total * REC, LANES), I32))


def _expert_kernel(te_ref, used_ref, rec_ref, wg_ref, wu_ref, wd_ref, y_ref, wg, wu, wd, *, tm):
    i = pl.program_id(0)
    prev = te_ref[jnp.maximum(i - 1, 0)]

    @pl.when((i == 0) | (te_ref[i] != prev))
    def _():
        wg[...] = wg_ref[0].astype(BF16)
        wu[...] = wu_ref[0].astype(BF16)
        wd[...] = wd_ref[0].astype(BF16)

    @pl.when(i < used_ref[0])
    def _():
        low, high = [], []
        for j in range(REC_W1):
            words = _strided(rec_ref, j, tm)[...]
            low.append(lax.bitcast_convert_type(lax.shift_left(words, 16), F32))
            high.append(lax.bitcast_convert_type(words & HI16, F32))
        t = jnp.concatenate(low + high, axis=1).astype(BF16)
        w1 = lax.bitcast_convert_type(_strided(rec_ref, REC_W1, tm)[...], F32)
        w2 = lax.bitcast_convert_type(_strided(rec_ref, REC_W2, tm)[...], F32)
        gate = jnp.where(_strided(rec_ref, REC_E1, tm)[...] == te_ref[i], w1, w2)
        a = jnp.dot(t, wg[...], preferred_element_type=F32)
        c = jnp.dot(t, wu[...], preferred_element_type=F32)
        gate = jnp.concatenate([gate] * (a.shape[1] // LANES), axis=1)
        hid = a * (1.0 / (1.0 + jnp.exp(-a))) * c * gate
        y = _mm(hid, wd[...])
        for j in range(y.shape[1] // LANES):
            _strided(y_ref, j, tm)[...] = y[:, j * LANES:(j + 1) * LANES]

    @pl.when(i >= used_ref[0])
    def _():
        y_ref[...] = jnp.zeros_like(y_ref)


def moe_experts(rec_sorted, tile_expert, n_used, w_gate, w_up, w_down, tm):
    p = rec_sorted.shape[0] // REC
    _, d, f = w_gate.shape
    assert p % tm == 0 and d == REC * LANES and f % LANES == 0
    grid_spec = pltpu.PrefetchScalarGridSpec(
        num_scalar_prefetch=2, grid=(p // tm,),
        in_specs=[pl.BlockSpec((tm * REC, LANES), lambda i, te, nu: (i, 0)),
                  pl.BlockSpec((1, d, f), lambda i, te, nu: (te[i], 0, 0)),
                  pl.BlockSpec((1, d, f), lambda i, te, nu: (te[i], 0, 0)),
                  pl.BlockSpec((1, f, d), lambda i, te, nu: (te[i], 0, 0))],
        out_specs=pl.BlockSpec((tm * REC, LANES), lambda i, te, nu: (i, 0)),
        scratch_shapes=[pltpu.VMEM((d, f), BF16), pltpu.VMEM((d, f), BF16), pltpu.VMEM((f, d), BF16)])
    return pl.pallas_call(
        functools.partial(_expert_kernel, tm=tm), grid_spec=grid_spec,
        out_shape=jax.ShapeDtypeStruct((p * REC, LANES), F32),
        compiler_params=_cparams(1))(tile_expert, n_used, rec_sorted, w_gate, w_up, w_down)


def _combine_kernel(idx_ref, nxt_ref, x_ref, y_hbm, out_ref, buf, sem, *, rows):
    i = pl.program_id(0)
    slot = i % 2
    count = TOP_K * rows

    def issue(ref, s):
        def body(r, c):
            pltpu.make_async_copy(_record(y_hbm, ref[0, 0, r]), _record(buf.at[s], r), sem.at[s]).start()
            return c
        lax.fori_loop(0, count, body, 0, unroll=ISSUE_UNROLL)

    @pl.when(i == 0)
    def _():
        issue(idx_ref, 0)

    @pl.when(i + 1 < pl.num_programs(0))
    def _():
        issue(nxt_ref, 1 - slot)

    pltpu.make_async_copy(y_hbm.at[pl.ds(0, count * REC)], buf.at[slot], sem.at[slot]).wait()
    for j in range(REC):
        cols = slice(j * LANES, (j + 1) * LANES)
        first = buf[slot, pl.ds(j, rows, stride=REC), :]
        second = buf[slot, pl.ds(rows * REC + j, rows, stride=REC), :]
        out_ref[:, cols] = x_ref[:, cols] + first + second


def moe_combine(x, y_rec, pos, rows):
    n, d = x.shape
    rows = min(rows, n)
    assert n % rows == 0 and y_rec.shape[0] >= TOP_K * rows * REC and d == REC * LANES
    nt = n // rows
    idx3 = pos.reshape(nt, rows, TOP_K).transpose(0, 2, 1).reshape(nt, 1, TOP_K * rows)
    return pl.pallas_call(
        functools.partial(_combine_kernel, rows=rows), grid=(nt,),
        in_specs=_tile_index_specs(nt, TOP_K * rows)
        + [pl.BlockSpec((rows, d), lambda i: (i, 0)), pl.BlockSpec(memory_space=pl.ANY)],
        out_specs=pl.BlockSpec((rows, d), lambda i: (i, 0)),
        out_shape=jax.ShapeDtypeStruct((n, d), F32),
        scratch_shapes=[pltpu.VMEM((2, TOP_K * rows * REC, LANES), F32), pltpu.SemaphoreType.DMA((2,))],
        compiler_params=_cparams(1))(idx3, idx3, x, y_rec)


PLAN_BLOCK = 256


def _moe_plan(route_e, tm):
    n = route_e.shape[0]
    e = route_e[:, :TOP_K].reshape(-1)
    na = e.shape[0]
    blk = min(PLAN_BLOCK, na)
    assert na % blk == 0
    onehot = (e[:, None] == jnp.arange(N_EXPERTS, dtype=I32)[None, :]).astype(I32).reshape(na // blk, blk, N_EXPERTS)
    within = jnp.cumsum(onehot, axis=1)
    block_tot = within[:, -1, :]
    before = jnp.cumsum(block_tot, axis=0) - block_tot
    counts = jnp.sum(block_tot, axis=0)
    padded = ((counts + tm - 1) // tm) * tm
    pad_end = jnp.cumsum(padded)
    pad_start = pad_end - padded
    slot_if = within - 1 + before[:, None, :] + pad_start[None, None, :]
    pos = jnp.sum(onehot * slot_if, axis=2).reshape(n, TOP_K)
    total = ((na + tm - 1) // tm) * tm + N_EXPERTS * tm
    tile_start = jnp.arange(total // tm, dtype=I32) * tm
    tile_expert = jnp.minimum(jnp.searchsorted(pad_end, tile_start, side="right"), N_EXPERTS - 1).astype(I32)
    n_used = (pad_end[-1:] // tm).astype(I32)
    return pos, tile_expert, n_used, total


def hier_moe_residual(x, g, wg, bg, wr, br, w_gate, w_up, w_down, tm_tokens, tm_expert):
    rec, route_e = moe_route(x, g, wg, bg, wr, br, tm_tokens)
    pos, tile_expert, n_used, total = _moe_plan(route_e, tm_expert)
    rec_sorted = dispatch_records(rec, pos, total, tm_expert)
    y_rec = moe_experts(rec_sorted, tile_expert, n_used, w_gate, w_up, w_down, tm_expert)
    return moe_combine(x, y_rec, pos, tm_expert)


def _pad_rows(x):
    b, c = x.shape
    return jnp.zeros((b, SAMPLE_ROWS, c), x.dtype).at[:, 0].set(x).reshape(b * SAMPLE_ROWS, c)


def _unpad_rows(x):
    return x.reshape(-1, SAMPLE_ROWS, x.shape[1])[:, 0]


def kernel(x_prompt, x_sample, mem_prompt, cache_sb_k, cache_sb_v, cache_mem_k, cache_mem_v, page_table, norm_mix_g, norm_ffn_g, norm_mem_g, w_mem_kv, w_in_sb, sb_bias, w_in_gm, gm_norm_g, gm_ws_tril, gm_b, w_out, w_router_grp, b_router_grp, w_router_exp, b_router_exp, w_gate, w_up, w_down, final_norm_g):
    batch, seq, d = x_prompt.shape
    dec_batch, dec_seq, _ = x_sample.shape
    assert dec_seq == 1
    n_mem = mem_prompt.shape[1]
    depth = w_out.shape[0]
    rows, cols = np.tril_indices(CHUNK)

    xp = x_prompt.reshape(batch * seq, d)
    xs = x_sample.reshape(dec_batch, d)
    memp = mem_prompt.reshape(batch * n_mem, d)
    sbk_p, sbv_p, sbk_s, sbv_s, memk_p, memv_p, gmv_s = [], [], [], [], [], [], []

    for l in range(depth):
        i = l // 2
        mk_p, mv_p = norm_matmul(memp, norm_mem_g[l], [w_mem_kv[l][:, :MEM_WIDTH], w_mem_kv[l][:, MEM_WIDTH:]], 512)
        memk_p.append(mk_p.reshape(batch, n_mem, N_MEM_HEADS, HEAD_DIM))
        memv_p.append(mv_p.reshape(batch, n_mem, N_MEM_HEADS, HEAD_DIM))
        mk_p = mk_p.reshape(batch, n_mem, MEM_WIDTH)
        mv_p = mv_p.reshape(batch, n_mem, MEM_WIDTH)
        mk_s = cache_mem_k[l].reshape(dec_batch, n_mem, MEM_WIDTH)
        mv_s = cache_mem_v[l].reshape(dec_batch, n_mem, MEM_WIDTH)
        xs8 = _pad_rows(xs)
        if l % 2 == 0:
            w = w_in_sb[i]
            ws = [w[:, :SB_WIDTH], w[:, SB_WIDTH:2 * SB_WIDTH], w[:, 2 * SB_WIDTH:3 * SB_WIDTH], w[:, 3 * SB_WIDTH:]]
            q, k, v, qm = norm_matmul(xp, norm_mix_g[l], ws, 512)
            sbk_p.append(k.reshape(batch, seq, N_SB_HEADS, HEAD_DIM))
            sbv_p.append(v.reshape(batch, seq, N_SB_HEADS, HEAD_DIM))
            o_p = sb_attention_prompt(q, k, v, sb_bias[i], batch, seq)
            qs, ks, vs, qms = norm_matmul(xs8, norm_mix_g[l], ws, 512)
            sbk_s.append(_unpad_rows(ks).reshape(dec_batch, 1, N_SB_HEADS, HEAD_DIM))
            sbv_s.append(_unpad_rows(vs).reshape(dec_batch, 1, N_SB_HEADS, HEAD_DIM))
            o_s = sb_attention_sample(_unpad_rows(qs), _pages_channel_major(cache_sb_k[i]),
                                      _pages_channel_major(cache_sb_v[i]), page_table, sb_bias[i])
            o_s = _pad_rows(o_s)
        else:
            w = w_in_gm[i]
            ws = [w[:, :GM_WIDTH], w[:, GM_WIDTH:2 * GM_WIDTH], w[:, 2 * GM_WIDTH:]]
            ws_full = jnp.zeros((GM_GROUPS, CHUNK, CHUNK), F32).at[:, rows, cols].set(gm_ws_tril[i])
            pu, pv, qm = norm_matmul(xp, norm_mix_g[l], ws, 512)
            o_p, _ = gm_gate(pu, pv, gm_norm_g[i], ws_full, gm_b[i], CHUNK)
            pus, pvs, qms = norm_matmul(xs8, norm_mix_g[l], ws, 512)
            o_s, v_s = gm_gate(pus, pvs, gm_norm_g[i], ws_full[:, :SAMPLE_ROWS, :SAMPLE_ROWS],
                               gm_b[i][:, :SAMPLE_ROWS], SAMPLE_ROWS)
            gmv_s.append(_unpad_rows(v_s).reshape(dec_batch, 1, GM_WIDTH))
        xp = mixer_out(xp, o_p, qm, mk_p, mv_p, w_out[l], seq, 512)
        xs = _unpad_rows(mixer_out(xs8, o_s, qms, mk_s, mv_s, w_out[l], SAMPLE_ROWS, SAMPLE_ROWS))
        moe_w = (norm_ffn_g[l], w_router_grp[l], b_router_grp[l], w_router_exp[l], b_router_exp[l],
                 w_gate[l], w_up[l], w_down[l])
        xp = hier_moe_residual(xp, *moe_w, 512, 256)
        xs = hier_moe_residual(xs, *moe_w, 32, 16)

    y_prompt = rmsnorm_rows(xp, final_norm_g, 1024).reshape(batch, seq, d)
    y_sample = rmsnorm_rows(xs, final_norm_g, 32).reshape(dec_batch, 1, d)
    return (y_prompt, y_sample, jnp.stack(sbk_p), jnp.stack(sbv_p), jnp.stack(sbk_s), jnp.stack(sbv_s),
            jnp.stack(memk_p), jnp.stack(memv_p), jnp.stack(gmv_s))
```

```python
import functools

import numpy as np
import jax
import jax.numpy as jnp
from jax import lax
from jax.experimental import pallas as pl
from jax.experimental.pallas import tpu as pltpu

F32 = jnp.float32
BF16 = jnp.bfloat16
I32 = jnp.int32

HEAD_DIM = 64
LANES = 128
SUBLANES = 8
N_SB_HEADS = 12
SB_WIDTH = N_SB_HEADS * HEAD_DIM
N_MEM_HEADS = 4
MEM_WIDTH = N_MEM_HEADS * HEAD_DIM
GM_GROUPS = 4
GM_WIDTH = 768
GM_GROUP_DIM = GM_WIDTH // GM_GROUPS
CHUNK = 128
PAGE_SIZE = 128
N_GROUPS = 4
EXPERTS_PER_GROUP = 8
N_EXPERTS = N_GROUPS * EXPERTS_PER_GROUP
TOP_K = 2
RMS_EPS = 1e-6
ATTN_SCALE = HEAD_DIM ** -0.5

SB_TK = 128
SB_TQ = 256
SB_STEP_HEADS = 4
PAGES_PER_STEP = 16
SAMPLE_ROWS = 2 * SUBLANES
VMEM_LIMIT = 48 * 1024 * 1024
NT_DIMS = (((1,), (1,)), ((), ()))
ISSUE_UNROLL = 8


def _cparams(n_axes):
    return pltpu.CompilerParams(dimension_semantics=("arbitrary",) * n_axes,
                                vmem_limit_bytes=VMEM_LIMIT)


def _mm(a, b):
    return jnp.dot(a.astype(BF16), b.astype(BF16), preferred_element_type=F32)


def _mm_nt(a, b):
    return lax.dot_general(a.astype(BF16), b.astype(BF16), NT_DIMS, preferred_element_type=F32)


def _split_bf16(x):
    hi = x.astype(BF16)
    lo = (x - hi.astype(F32)).astype(BF16)
    return hi, lo


def _round_bf16(x):
    return x.astype(BF16).astype(F32)


def _rms(x, g):
    return x * lax.rsqrt(jnp.mean(x * x, axis=-1, keepdims=True) + RMS_EPS) * g


def _gelu(x):
    return x * (0.5 * (1.0 + jnp.tanh(np.sqrt(2.0 / np.pi).astype(np.float32) * (x + 0.044715 * (x * x * x)))))


def _log_sigmoid_pair(z):
    l = jnp.log(1.0 + jnp.exp(-jnp.abs(z)))
    return jnp.minimum(z, 0.0) - l, -(jnp.maximum(z, 0.0) + l)


def _cumsum_weights():
    j = np.arange(SB_TK)[:, None]
    s = np.arange(SB_TK)[None, :]
    w = np.concatenate([(j > s).astype(np.float32), np.ones((SB_TK, SB_TK), np.float32)], axis=1)
    return jnp.asarray(np.concatenate([w, w], axis=0), dtype=BF16)


def _later_sums(log_keep, w2):
    hi, lo = _split_bf16(log_keep)
    return jnp.dot(jnp.concatenate([hi, lo], axis=1), w2, preferred_element_type=F32)


def _norm_matmul_kernel(x_ref, g_ref, *refs, n_out):
    h = _rms(x_ref[...], g_ref[...]).astype(BF16)
    for w_ref, o_ref in zip(refs[:n_out], refs[n_out:]):
        o_ref[...] = jnp.dot(h, w_ref[...], preferred_element_type=F32)


def norm_matmul(x, g, ws, tm):
    n, d = x.shape
    tm = min(tm, n)
    assert n % tm == 0
    ws = [w.astype(BF16) for w in ws]
    in_specs = [pl.BlockSpec((tm, d), lambda i: (i, 0)), pl.BlockSpec((1, d), lambda i: (0, 0))]
    in_specs += [pl.BlockSpec(w.shape, lambda i: (0, 0)) for w in ws]
    out_specs = [pl.BlockSpec((tm, w.shape[1]), lambda i: (i, 0)) for w in ws]
    out_shape = [jax.ShapeDtypeStruct((n, w.shape[1]), F32) for w in ws]
    return pl.pallas_call(
        functools.partial(_norm_matmul_kernel, n_out=len(ws)),
        grid=(n // tm,), in_specs=in_specs, out_specs=out_specs, out_shape=out_shape,
        compiler_params=_cparams(1))(x, g.reshape(1, d), *ws)


def _rmsnorm_kernel(x_ref, g_ref, o_ref):
    o_ref[...] = _rms(x_ref[...], g_ref[...])


def rmsnorm_rows(x, g, tm):
    n, d = x.shape
    tm = min(tm, n)
    assert n % tm == 0
    return pl.pallas_call(
        _rmsnorm_kernel, grid=(n // tm,),
        in_specs=[pl.BlockSpec((tm, d), lambda i: (i, 0)), pl.BlockSpec((1, d), lambda i: (0, 0))],
        out_specs=pl.BlockSpec((tm, d), lambda i: (i, 0)),
        out_shape=jax.ShapeDtypeStruct((n, d), F32), compiler_params=_cparams(1))(x, g.reshape(1, d))


def _sb_prompt_kernel(bias_ref, q_ref, k_ref, v_ref, w2_ref, o_ref, kb_ref, vb_ref, acc_ref, later_ref):
    hg = pl.program_id(1)
    qi = pl.program_id(2)

    @pl.when(qi == 0)
    def _():
        kb_ref[...] = k_ref[...].astype(BF16)
        vb_ref[...] = v_ref[...].astype(BF16)

    lane = lax.broadcasted_iota(I32, (SB_TQ, LANES), 1)
    half_masks = (lane < HEAD_DIM, lane >= HEAD_DIM)
    q = q_ref[...] * ATTN_SCALE
    blocks = [slice((h // 2) * LANES, (h // 2 + 1) * LANES) for h in range(SB_STEP_HEADS)]
    qh = [jnp.where(half_masks[h % 2], q[:, blocks[h]], 0.0).astype(BF16) for h in range(SB_STEP_HEADS)]
    bias = [bias_ref[SB_STEP_HEADS * hg + h] for h in range(SB_STEP_HEADS)]
    w2 = w2_ref[...]
    row = lax.broadcasted_iota(I32, (SB_TQ, SB_TQ), 0)
    col = lax.broadcasted_iota(I32, (SB_TQ, SB_TQ), 1)
    strictly_before = col < row
    acc_ref[...] = jnp.zeros_like(acc_ref)
    later_ref[...] = jnp.zeros_like(later_ref)

    def key_step(j, diagonal):
        start = pl.multiple_of(j * SB_TQ, SB_TQ)
        for h in range(SB_STEP_HEADS):
            kj = kb_ref[pl.ds(start, SB_TQ), blocks[h]]
            vj = vb_ref[pl.ds(start, SB_TQ), blocks[h]]
            z = lax.dot_general(qh[h], kj, NT_DIMS, preferred_element_type=F32) + bias[h]
            log_beta = jnp.minimum(z, 0.0) - jnp.log(1.0 + jnp.exp(-jnp.abs(z)))
            log_keep = log_beta - z
            if diagonal:
                log_keep = jnp.where(strictly_before, log_keep, 0.0)
            later = later_ref[h]
            parts = [None] * (SB_TQ // SB_TK)
            for s in reversed(range(SB_TQ // SB_TK)):
                cols = slice(s * SB_TK, (s + 1) * SB_TK)
                sums = _later_sums(log_keep[:, cols], w2)
                parts[s] = jnp.exp(log_beta[:, cols] + sums[:, :SB_TK] + later)
                later = later + sums[:, SB_TK:]
            a = jnp.concatenate(parts, axis=1)
            if diagonal:
                a = jnp.where(strictly_before, a, 0.0)
            acc_ref[h] += jnp.dot(a.astype(BF16), vj, preferred_element_type=F32)
            later_ref[h] = later

    key_step(qi, True)

    def body(t, c):
        key_step(qi - 1 - t, False)
        return c
    lax.fori_loop(0, qi, body, 0)
    for b in range(SB_STEP_HEADS // 2):
        o_ref[:, b * LANES:(b + 1) * LANES] = jnp.where(half_masks[0], acc_ref[2 * b], acc_ref[2 * b + 1])


def sb_attention_prompt(q, k, v, bias, batch, seq):
    n = q.shape[0]
    assert seq % SB_TQ == 0 and SB_TQ % SB_TK == 0 and n == batch * seq and N_SB_HEADS % SB_STEP_HEADS == 0
    nq = seq // SB_TQ
    width = SB_STEP_HEADS * HEAD_DIM
    return pl.pallas_call(
        _sb_prompt_kernel,
        grid=(batch, N_SB_HEADS // SB_STEP_HEADS, nq),
        in_specs=[
            pl.BlockSpec(memory_space=pltpu.SMEM),
            pl.BlockSpec((SB_TQ, width), lambda b, hg, qi: (b * nq + qi, hg)),
            pl.BlockSpec((seq, width), lambda b, hg, qi: (b, hg)),
            pl.BlockSpec((seq, width), lambda b, hg, qi: (b, hg)),
            pl.BlockSpec((2 * SB_TK, 2 * SB_TK), lambda b, hg, qi: (0, 0)),
        ],
        out_specs=pl.BlockSpec((SB_TQ, width), lambda b, hg, qi: (b * nq + qi, hg)),
        out_shape=jax.ShapeDtypeStruct((n, SB_WIDTH), F32),
        scratch_shapes=[pltpu.VMEM((seq, width), BF16), pltpu.VMEM((seq, width), BF16),
                        pltpu.VMEM((SB_STEP_HEADS, SB_TQ, LANES), F32),
                        pltpu.VMEM((SB_STEP_HEADS, SB_TQ, LANES), F32)],
        compiler_params=_cparams(3))(bias, q, k, v, _cumsum_weights())


SB_HEAD_ROWS = 16


def _sb_sample_kernel(pt_ref, q_ref, bias_ref, w2_ref, *refs):
    del pt_ref
    k_refs = refs[:PAGES_PER_STEP]
    v_refs = refs[PAGES_PER_STEP:2 * PAGES_PER_STEP]
    o_ref = refs[2 * PAGES_PER_STEP]
    qcol_ref, z_ref, a_ref, acc_ref, later_ref = refs[2 * PAGES_PER_STEP + 1:]
    j = pl.program_id(1)

    @pl.when(j == 0)
    def _():
        acc_ref[...] = jnp.zeros_like(acc_ref)
        later_ref[...] = jnp.zeros_like(later_ref)
        z_ref[...] = jnp.zeros_like(z_ref)
        qcol_ref[...] = _round_bf16(jnp.broadcast_to(q_ref[0] * ATTN_SCALE, qcol_ref.shape))

    heads = [slice(h * HEAD_DIM, (h + 1) * HEAD_DIM) for h in range(N_SB_HEADS)]
    for i in range(PAGES_PER_STEP):
        for h, rows in enumerate(heads):
            prod = _round_bf16(k_refs[i][0, rows, :]) * qcol_ref[rows, :]
            z_ref[pl.ds(i * SB_HEAD_ROWS + h, 1), :] = jnp.sum(prod, axis=0, keepdims=True)
    log_beta, log_keep = _log_sigmoid_pair(z_ref[...] + bias_ref[...])
    sums = _later_sums(log_keep, w2_ref[...])
    later = later_ref[...]
    for i in range(PAGES_PER_STEP):
        r = slice(i * SB_HEAD_ROWS, (i + 1) * SB_HEAD_ROWS)
        a_ref[r, :] = _round_bf16(jnp.exp(log_beta[r] + sums[r, :PAGE_SIZE] + later))
        later = later + sums[r, PAGE_SIZE:]
    later_ref[...] = later
    for h, rows in enumerate(heads):
        acc = acc_ref[rows, :]
        for i in range(PAGES_PER_STEP):
            acc = acc + _round_bf16(v_refs[i][0, rows, :]) * a_ref[pl.ds(i * SB_HEAD_ROWS + h, 1), :]
        acc_ref[rows, :] = acc

    @pl.when(j == pl.num_programs(1) - 1)
    def _():
        o_ref[0] = jnp.sum(acc_ref[...], axis=1, keepdims=True)


def sb_attention_sample(q, cache_kt, cache_vt, page_table, bias):
    nb, n_pages = page_table.shape
    assert n_pages % PAGES_PER_STEP == 0 and PAGE_SIZE == SB_TK
    steps = n_pages // PAGES_PER_STEP
    bias_rows = jnp.zeros((SB_HEAD_ROWS, PAGE_SIZE), F32).at[:N_SB_HEADS].set(
        jnp.broadcast_to(bias[:, None], (N_SB_HEADS, PAGE_SIZE)))
    bias_rows = jnp.tile(bias_rows, (PAGES_PER_STEP, 1))
    step_rows = PAGES_PER_STEP * SB_HEAD_ROWS

    def page_spec(i):
        return pl.BlockSpec((1, SB_WIDTH, PAGE_SIZE),
                            lambda b, j, pt: (pt[b, n_pages - 1 - (j * PAGES_PER_STEP + i)], 0, 0))

    grid_spec = pltpu.PrefetchScalarGridSpec(
        num_scalar_prefetch=1, grid=(nb, steps),
        in_specs=[pl.BlockSpec((1, SB_WIDTH, 1), lambda b, j, pt: (b, 0, 0)),
                  pl.BlockSpec((step_rows, PAGE_SIZE), lambda b, j, pt: (0, 0)),
                  pl.BlockSpec((2 * SB_TK, 2 * SB_TK), lambda b, j, pt: (0, 0))]
        + [page_spec(i) for i in range(PAGES_PER_STEP)] * 2,
        out_specs=pl.BlockSpec((1, SB_WIDTH, 1), lambda b, j, pt: (b, 0, 0)),
        scratch_shapes=[pltpu.VMEM((SB_WIDTH, PAGE_SIZE), F32),
                        pltpu.VMEM((step_rows, PAGE_SIZE), F32),
                        pltpu.VMEM((step_rows, PAGE_SIZE), F32),
                        pltpu.VMEM((SB_WIDTH, PAGE_SIZE), F32),
                        pltpu.VMEM((SB_HEAD_ROWS, PAGE_SIZE), F32)])
    out = pl.pallas_call(
        _sb_sample_kernel, grid_spec=grid_spec,
        out_shape=jax.ShapeDtypeStruct((nb, SB_WIDTH, 1), F32),
        compiler_params=_cparams(2))(
            page_table, q.reshape(nb, SB_WIDTH, 1), bias_rows, _cumsum_weights(),
            *([cache_kt] * PAGES_PER_STEP), *([cache_vt] * PAGES_PER_STEP))
    return out.reshape(nb, SB_WIDTH)


def _pages_channel_major(cache):
    n_phys = cache.shape[0]
    return jnp.transpose(cache, (0, 2, 3, 1)).reshape(n_phys, SB_WIDTH, PAGE_SIZE)


def _gm_kernel(x_ref, g_ref, wu_ref, wv_ref, wq_ref, vg_ref, ws_ref, bias_ref, s_ref, qm_ref, *v_out, chunk):
    h = _rms(x_ref[...], g_ref[...]).astype(BF16)
    qm_ref[...] = jnp.dot(h, wq_ref[...], preferred_element_type=F32)
    u = _gelu(jnp.dot(h, wu_ref[...], preferred_element_type=F32))
    v = _rms(_gelu(jnp.dot(h, wv_ref[...], preferred_element_type=F32)), vg_ref[...])
    if v_out:
        v_out[0][...] = v
    vb = v.astype(BF16)
    group = lax.broadcasted_iota(I32, (chunk, GM_WIDTH), 1) // GM_GROUP_DIM
    for c in range(v.shape[0] // chunk):
        rows = slice(c * chunk, (c + 1) * chunk)
        mixed = bias_ref[...]
        for g in range(GM_GROUPS):
            mixed = mixed + jnp.where(group == g, jnp.dot(ws_ref[g], vb[rows], preferred_element_type=F32), 0.0)
        s_ref[rows, :] = u[rows] * mixed


def gm_mixer_in(x, g, w_in, vg, ws_full, bias, chunk, tm, emit_v):
    n, d = x.shape
    tm = min(tm, n)
    assert n % tm == 0 and tm % chunk == 0
    bias_full = jnp.repeat(bias.T, GM_GROUP_DIM, axis=1)
    w = w_in.astype(BF16)
    ws = [w[:, :GM_WIDTH], w[:, GM_WIDTH:2 * GM_WIDTH], w[:, 2 * GM_WIDTH:]]
    row = lambda i: (i, 0)
    fixed = lambda i: (0, 0)
    widths = [GM_WIDTH, MEM_WIDTH] + ([GM_WIDTH] if emit_v else [])
    return pl.pallas_call(
        functools.partial(_gm_kernel, chunk=chunk), grid=(n // tm,),
        in_specs=[pl.BlockSpec((tm, d), row), pl.BlockSpec((1, d), fixed)]
        + [pl.BlockSpec(wi.shape, fixed) for wi in ws]
        + [pl.BlockSpec((1, GM_WIDTH), fixed), pl.BlockSpec((GM_GROUPS, chunk, chunk), lambda i: (0, 0, 0)),
           pl.BlockSpec((chunk, GM_WIDTH), fixed)],
        out_specs=[pl.BlockSpec((tm, wd), row) for wd in widths],
        out_shape=[jax.ShapeDtypeStruct((n, wd), F32) for wd in widths],
        compiler_params=_cparams(1))(x, g.reshape(1, d), *ws, vg.reshape(1, GM_WIDTH), ws_full.astype(BF16), bias_full)


def _mixer_out_kernel(x_ref, o_ref, qm_ref, mk_ref, mv_ref, wo_ref, out_ref):
    qm = qm_ref[...] * ATTN_SCALE
    mk = mk_ref[0].astype(BF16)
    mv = mv_ref[0].astype(BF16)
    head = lax.broadcasted_iota(I32, qm.shape, 1) // HEAD_DIM
    o_mem = jnp.zeros(qm.shape, F32)
    for h in range(N_MEM_HEADS):
        own = head == h
        s = _mm_nt(jnp.where(own, qm, 0.0), mk)
        p = jnp.exp(s - jnp.max(s, axis=-1, keepdims=True))
        p = p / jnp.sum(p, axis=-1, keepdims=True)
        o_mem = jnp.where(own, _mm(p, mv), o_mem)
    wo = wo_ref[...]
    width = o_ref.shape[1]
    out_ref[...] = x_ref[...] + _mm(o_ref[...], wo[:width]) + _mm(o_mem, wo[width:])


def mixer_out(x, o_mix, qm, mk, mv, wo, rows_per_batch, tm):
    n, d = x.shape
    tm = min(tm, rows_per_batch)
    assert rows_per_batch % tm == 0 and n % tm == 0
    per = rows_per_batch // tm
    wo = wo.astype(BF16)
    row = lambda i: (i, 0)
    mem = lambda i: (i // per, 0, 0)
    return pl.pallas_call(
        _mixer_out_kernel, grid=(n // tm,),
        in_specs=[pl.BlockSpec((tm, d), row), pl.BlockSpec((tm, o_mix.shape[1]), row),
                  pl.BlockSpec((tm, MEM_WIDTH), row),
                  pl.BlockSpec((1,) + mk.shape[1:], mem), pl.BlockSpec((1,) + mv.shape[1:], mem),
                  pl.BlockSpec(wo.shape, lambda i: (0, 0))],
        out_specs=pl.BlockSpec((tm, d), row),
        out_shape=jax.ShapeDtypeStruct((n, d), F32),
        compiler_params=_cparams(1))(x, o_mix, qm, mk, mv, wo)


ROUTE_LANES = LANES
REC = SUBLANES


def _strided(ref, first, count):
    return ref.at[pl.ds(first, count, stride=REC), :]


def _route_kernel(x_ref, g_ref, w_ref, b_ref, tri_ref, rec_ref, re_ref, rw_ref, cnt_ref, seen_ref):
    t = _rms(x_ref[...], g_ref[...])
    tm, d = t.shape
    logits = _mm(t, w_ref[...]) + b_ref[...]
    lane = lax.broadcasted_iota(I32, logits.shape, 1).astype(F32)
    neg = jnp.float32(-jnp.inf)
    far = jnp.float32(ROUTE_LANES)

    def first_max(vals):
        m = jnp.max(vals, axis=-1, keepdims=True)
        return m, jnp.min(jnp.where(vals == m, lane, far), axis=-1, keepdims=True)

    is_group = lane < N_GROUPS
    gmax, gsel = first_max(jnp.where(is_group, logits, neg))
    p_grp = 1.0 / jnp.sum(jnp.where(is_group, jnp.exp(logits - gmax), 0.0), axis=-1, keepdims=True)
    lo = N_GROUPS + gsel * EXPERTS_PER_GROUP
    cand = jnp.where((lane >= lo) & (lane < lo + EXPERTS_PER_GROUP), logits, neg)
    m1, i1 = first_max(cand)
    m2, i2 = first_max(jnp.where(lane == i1, neg, cand))
    e2 = jnp.exp(m2 - m1)
    den = 1.0 + e2
    w1 = (1.0 / den) * p_grp
    w2 = (e2 / den) * p_grp
    ex1 = (i1 - N_GROUPS).astype(I32)
    ex2 = (i2 - N_GROUPS).astype(I32)

    @pl.when(pl.program_id(0) == 0)
    def _():
        seen_ref[...] = jnp.zeros_like(seen_ref)

    pick1 = lane == i1
    pick2 = lane == i2
    chosen = jnp.where(pick1 | pick2, 1.0, 0.0)
    before = jnp.dot(tri_ref[...], chosen.astype(BF16), preferred_element_type=F32) + seen_ref[...]
    rank1 = jnp.sum(jnp.where(pick1, before, 0.0), axis=-1, keepdims=True).astype(I32)
    rank2 = jnp.sum(jnp.where(pick2, before, 0.0), axis=-1, keepdims=True).astype(I32)
    seen_ref[...] += jnp.sum(chosen, axis=0, keepdims=True)
    cnt_ref[...] = seen_ref[...]
    re_ref[...] = jnp.where(lane == 0, ex1, jnp.where(lane == 1, ex2,
                            jnp.where(lane == 2, rank1, jnp.where(lane == 3, rank2, 0))))

    rw_ref[...] = jnp.where(lane == 0, w1, jnp.where(lane == 1, w2, 0.0))
    for j in range(d // LANES):
        _strided(rec_ref, j, tm)[...] = t[:, j * LANES:(j + 1) * LANES]


def moe_route(x, g, wg, bg, wr, br, tm):
    n, d = x.shape
    tm = min(tm, n)
    assert n % tm == 0 and d == REC * LANES
    pad = ROUTE_LANES - N_GROUPS - N_EXPERTS
    w = jnp.concatenate([wg, wr, jnp.zeros((d, pad), F32)], axis=1).astype(BF16)
    b = jnp.concatenate([bg, br, jnp.zeros((pad,), F32)]).reshape(1, ROUTE_LANES)
    earlier_rows = jnp.asarray(np.tril(np.ones((tm, tm), np.float32), -1), dtype=BF16)
    row = lambda i: (i, 0)
    fixed = lambda i: (0, 0)
    return pl.pallas_call(
        _route_kernel, grid=(n // tm,),
        in_specs=[pl.BlockSpec((tm, d), row), pl.BlockSpec((1, d), fixed),
                  pl.BlockSpec((d, ROUTE_LANES), fixed), pl.BlockSpec((1, ROUTE_LANES), fixed),
                  pl.BlockSpec((tm, tm), fixed)],
        out_specs=[pl.BlockSpec((tm * REC, LANES), row), pl.BlockSpec((tm, ROUTE_LANES), row),
                   pl.BlockSpec((tm, ROUTE_LANES), row), pl.BlockSpec((1, ROUTE_LANES), fixed)],
        out_shape=[jax.ShapeDtypeStruct((n * REC, LANES), F32), jax.ShapeDtypeStruct((n, ROUTE_LANES), I32),
                   jax.ShapeDtypeStruct((n, ROUTE_LANES), F32), jax.ShapeDtypeStruct((1, ROUTE_LANES), F32)],
        scratch_shapes=[pltpu.VMEM((1, ROUTE_LANES), F32)],
        compiler_params=_cparams(1))(x, g.reshape(1, d), w, b, earlier_rows)


def _record(ref, index):
    return ref.at[pl.ds(pl.multiple_of(index * REC, REC), REC)]


def _tile_index_specs(nt, width):
    cur = pl.BlockSpec((1, 1, width), lambda i: (i, 0, 0), memory_space=pltpu.SMEM)
    nxt = pl.BlockSpec((1, 1, width), lambda i: (jnp.minimum(i + 1, nt - 1), 0, 0), memory_space=pltpu.SMEM)
    return [cur, nxt]


def _dispatch_kernel(pos_ref, rec_ref, init_hbm, dst_hbm, stage, sem, *, tokens):
    del init_hbm
    i = pl.program_id(0)
    last = pl.num_programs(0) - 1
    slot = i % 2

    def wait_step(s):
        for _ in range(TOP_K):
            pltpu.make_async_copy(stage.at[s], dst_hbm.at[pl.ds(0, tokens * REC)], sem.at[s]).wait()

    @pl.when(i >= 2)
    def _():
        wait_step(slot)

    stage[slot] = rec_ref[...]

    def body(r, c):
        src = _record(stage.at[slot], r)
        for k in range(TOP_K):
            pltpu.make_async_copy(src, _record(dst_hbm, pos_ref[0, 0, k * tokens + r]), sem.at[slot]).start()
        return c
    lax.fori_loop(0, tokens, body, 0, unroll=ISSUE_UNROLL)

    @pl.when(i == last)
    def _():
        wait_step(slot)

    @pl.when((i == last) & (i >= 1))
    def _():
        wait_step(1 - slot)


def dispatch_records(rec, pos, total, tokens):
    n = pos.shape[0]
    tokens = min(tokens, n)
    assert n % tokens == 0 and total >= tokens
    nt = n // tokens
    idx3 = pos.reshape(nt, tokens, TOP_K).transpose(0, 2, 1).reshape(nt, 1, TOP_K * tokens)
    return pl.pallas_call(
        functools.partial(_dispatch_kernel, tokens=tokens), grid=(nt,),
        in_specs=[pl.BlockSpec((1, 1, TOP_K * tokens), lambda i: (i, 0, 0), memory_space=pltpu.SMEM),
                  pl.BlockSpec((tokens * REC, LANES), lambda i: (i, 0)), pl.BlockSpec(memory_space=pl.ANY)],
        out_specs=pl.BlockSpec(memory_space=pl.ANY),
        out_shape=jax.ShapeDtypeStruct((total * REC, LANES), F32),
        scratch_shapes=[pltpu.VMEM((2, tokens * REC, LANES), F32), pltpu.SemaphoreType.DMA((2,))],
        input_output_aliases={2: 0},
        compiler_params=_cparams(1))(idx3, rec, jnp.zeros((total * REC, LANES), F32))


def _expert_kernel(te_ref, used_ref, rec_ref, wg_ref, wu_ref, wd_ref, y_ref, wg, wu, wd, *, tm):
    i = pl.program_id(0)
    prev = te_ref[jnp.maximum(i - 1, 0)]

    @pl.when((i == 0) | (te_ref[i] != prev))
    def _():
        wg[...] = wg_ref[0].astype(BF16)
        wu[...] = wu_ref[0].astype(BF16)
        wd[...] = wd_ref[0].astype(BF16)

    @pl.when(i < used_ref[0])
    def _():
        t = jnp.concatenate([_strided(rec_ref, j, tm)[...] for j in range(REC)], axis=1).astype(BF16)
        a = jnp.dot(t, wg[...], preferred_element_type=F32)
        c = jnp.dot(t, wu[...], preferred_element_type=F32)
        y = _mm(a * (1.0 / (1.0 + jnp.exp(-a))) * c, wd[...])
        for j in range(y.shape[1] // LANES):
            _strided(y_ref, j, tm)[...] = y[:, j * LANES:(j + 1) * LANES]

    @pl.when(i >= used_ref[0])
    def _():
        y_ref[...] = jnp.zeros_like(y_ref)


def moe_experts(rec_sorted, tile_expert, n_used, w_gate, w_up, w_down, tm):
    p = rec_sorted.shape[0] // REC
    _, d, f = w_gate.shape
    assert p % tm == 0 and d == REC * LANES and f % LANES == 0
    grid_spec = pltpu.PrefetchScalarGridSpec(
        num_scalar_prefetch=2, grid=(p // tm,),
        in_specs=[pl.BlockSpec((tm * REC, LANES), lambda i, te, nu: (i, 0)),
                  pl.BlockSpec((1, d, f), lambda i, te, nu: (te[i], 0, 0)),
                  pl.BlockSpec((1, d, f), lambda i, te, nu: (te[i], 0, 0)),
                  pl.BlockSpec((1, f, d), lambda i, te, nu: (te[i], 0, 0))],
        out_specs=pl.BlockSpec((tm * REC, LANES), lambda i, te, nu: (i, 0)),
        scratch_shapes=[pltpu.VMEM((d, f), BF16), pltpu.VMEM((d, f), BF16), pltpu.VMEM((f, d), BF16)])
    return pl.pallas_call(
        functools.partial(_expert_kernel, tm=tm), grid_spec=grid_spec,
        out_shape=jax.ShapeDtypeStruct((p * REC, LANES), F32),
        compiler_params=_cparams(1))(tile_expert, n_used, rec_sorted, w_gate, w_up, w_down)


def _combine_kernel(idx_ref, nxt_ref, x_ref, rw_ref, y_hbm, out_ref, buf, sem, *, rows):
    i = pl.program_id(0)
    slot = i % 2
    count = TOP_K * rows

    def issue(ref, s):
        def body(r, c):
            pltpu.make_async_copy(_record(y_hbm, ref[0, 0, r]), _record(buf.at[s], r), sem.at[s]).start()
            return c
        lax.fori_loop(0, count, body, 0, unroll=ISSUE_UNROLL)

    @pl.when(i == 0)
    def _():
        issue(idx_ref, 0)

    @pl.when(i + 1 < pl.num_programs(0))
    def _():
        issue(nxt_ref, 1 - slot)

    pltpu.make_async_copy(y_hbm.at[pl.ds(0, count * REC)], buf.at[slot], sem.at[slot]).wait()
    rw = rw_ref[...]
    w1 = jnp.broadcast_to(rw[:, 0:1], (rows, LANES))
    w2 = jnp.broadcast_to(rw[:, 1:2], (rows, LANES))
    for j in range(REC):
        cols = slice(j * LANES, (j + 1) * LANES)
        first = buf[slot, pl.ds(j, rows, stride=REC), :]
        second = buf[slot, pl.ds(rows * REC + j, rows, stride=REC), :]
        out_ref[:, cols] = x_ref[:, cols] + (w1 * first + w2 * second)


def moe_combine(x, y_rec, pos, route_w, rows):
    n, d = x.shape
    rows = min(rows, n)
    assert n % rows == 0 and y_rec.shape[0] >= TOP_K * rows * REC and d == REC * LANES
    nt = n // rows
    idx3 = pos.reshape(nt, rows, TOP_K).transpose(0, 2, 1).reshape(nt, 1, TOP_K * rows)
    return pl.pallas_call(
        functools.partial(_combine_kernel, rows=rows), grid=(nt,),
        in_specs=_tile_index_specs(nt, TOP_K * rows)
        + [pl.BlockSpec((rows, d), lambda i: (i, 0)), pl.BlockSpec((rows, ROUTE_LANES), lambda i: (i, 0)),
           pl.BlockSpec(memory_space=pl.ANY)],
        out_specs=pl.BlockSpec((rows, d), lambda i: (i, 0)),
        out_shape=jax.ShapeDtypeStruct((n, d), F32),
        scratch_shapes=[pltpu.VMEM((2, TOP_K * rows * REC, LANES), F32), pltpu.SemaphoreType.DMA((2,))],
        compiler_params=_cparams(1))(idx3, idx3, x, route_w, y_rec)


def _moe_plan(route_e, seen, tm):
    n = route_e.shape[0]
    experts = jnp.arange(N_EXPERTS, dtype=I32)
    counts = seen[0, N_GROUPS:N_GROUPS + N_EXPERTS].astype(I32)
    padded = ((counts + tm - 1) // tm) * tm
    pad_end = jnp.cumsum(padded)
    pad_start = pad_end - padded
    e = route_e[:, :TOP_K]
    rank = route_e[:, TOP_K:2 * TOP_K]
    pos = jnp.sum(jnp.where(e[:, :, None] == experts, pad_start, 0), axis=-1) + rank
    total = ((TOP_K * n + tm - 1) // tm) * tm + N_EXPERTS * tm
    tile_start = jnp.arange(total // tm, dtype=I32) * tm
    tile_expert = jnp.minimum(jnp.sum((tile_start[:, None] >= pad_end[None, :]).astype(I32), axis=1), N_EXPERTS - 1)
    n_used = (pad_end[-1:] // tm).astype(I32)
    return pos, tile_expert, n_used, total


def hier_moe_residual(x, g, wg, bg, wr, br, w_gate, w_up, w_down, tm_tokens, tm_expert):
    rec, route_e, route_w, seen = moe_route(x, g, wg, bg, wr, br, tm_tokens)
    pos, tile_expert, n_used, total = _moe_plan(route_e, seen, tm_expert)
    rec_sorted = dispatch_records(rec, pos, total, tm_expert)
    y_rec = moe_experts(rec_sorted, tile_expert, n_used, w_gate, w_up, w_down, tm_expert)
    return moe_combine(x, y_rec, pos, route_w, tm_expert)


def _moe_rows_kernel(e_ref, w_ref, x_ref, g_ref, wg_ref, wu_ref, wd_ref, o_ref):
    a_id = pl.program_id(0) * TOP_K + pl.program_id(1)
    x = x_ref[...]
    t = _rms(x, g_ref[...]).astype(BF16)
    a = jnp.dot(t, wg_ref[0].astype(BF16), preferred_element_type=F32)
    c = jnp.dot(t, wu_ref[0].astype(BF16), preferred_element_type=F32)
    y = _mm(a * (1.0 / (1.0 + jnp.exp(-a))) * c * w_ref[a_id], wd_ref[0])

    @pl.when(pl.program_id(1) == 0)
    def _():
        o_ref[...] = x + y

    @pl.when(pl.program_id(1) > 0)
    def _():
        o_ref[...] += y


def hier_moe_residual_rows(x, g, wg, bg, wr, br, w_gate, w_up, w_down, rows_per_token):
    n, d = x.shape
    nt = n // rows_per_token
    _, f = w_gate.shape[1:]
    _, route_e, route_w, _ = moe_route(x, g, wg, bg, wr, br, n)
    e = route_e.reshape(nt, rows_per_token, ROUTE_LANES)[:, 0, :TOP_K].reshape(-1)
    w = route_w.reshape(nt, rows_per_token, ROUTE_LANES)[:, 0, :TOP_K].reshape(-1)
    expert = lambda i, k, e_ref: (e_ref[i * TOP_K + k], 0, 0)
    grid_spec = pltpu.PrefetchScalarGridSpec(
        num_scalar_prefetch=1, grid=(nt, TOP_K),
        in_specs=[pl.BlockSpec(memory_space=pltpu.SMEM),
                  pl.BlockSpec((rows_per_token, d), lambda i, k, e_ref: (i, 0)),
                  pl.BlockSpec((1, d), lambda i, k, e_ref: (0, 0)),
                  pl.BlockSpec((1, d, f), expert), pl.BlockSpec((1, d, f), expert), pl.BlockSpec((1, f, d), expert)],
        out_specs=pl.BlockSpec((rows_per_token, d), lambda i, k, e_ref: (i, 0)))
    return pl.pallas_call(
        _moe_rows_kernel, grid_spec=grid_spec,
        out_shape=jax.ShapeDtypeStruct((n, d), F32),
        compiler_params=_cparams(2))(e, w, x, g.reshape(1, d), w_gate, w_up, w_down)


def _pad_rows(x):
    b, c = x.shape
    return jnp.zeros((b, SAMPLE_ROWS, c), x.dtype).at[:, 0].set(x).reshape(b * SAMPLE_ROWS, c)


def _unpad_rows(x):
    return x.reshape(-1, SAMPLE_ROWS, x.shape[1])[:, 0]


def kernel(x_prompt, x_sample, mem_prompt, cache_sb_k, cache_sb_v, cache_mem_k, cache_mem_v, page_table, norm_mix_g, norm_ffn_g, norm_mem_g, w_mem_kv, w_in_sb, sb_bias, w_in_gm, gm_norm_g, gm_ws_tril, gm_b, w_out, w_router_grp, b_router_grp, w_router_exp, b_router_exp, w_gate, w_up, w_down, final_norm_g):
    batch, seq, d = x_prompt.shape
    dec_batch, dec_seq, _ = x_sample.shape
    assert dec_seq == 1
    n_mem = mem_prompt.shape[1]
    depth = w_out.shape[0]
    rows, cols = np.tril_indices(CHUNK)

    xp = x_prompt.reshape(batch * seq, d)
    xs = x_sample.reshape(dec_batch, d)
    memp = mem_prompt.reshape(batch * n_mem, d)
    sbk_p, sbv_p, sbk_s, sbv_s, memk_p, memv_p, gmv_s = [], [], [], [], [], [], []

    for l in range(depth):
        i = l // 2
        mk_p, mv_p = norm_matmul(memp, norm_mem_g[l], [w_mem_kv[l][:, :MEM_WIDTH], w_mem_kv[l][:, MEM_WIDTH:]], 512)
        memk_p.append(mk_p.reshape(batch, n_mem, N_MEM_HEADS, HEAD_DIM))
        memv_p.append(mv_p.reshape(batch, n_mem, N_MEM_HEADS, HEAD_DIM))
        mk_p = mk_p.reshape(batch, n_mem, MEM_WIDTH)
        mv_p = mv_p.reshape(batch, n_mem, MEM_WIDTH)
        mk_s = cache_mem_k[l].reshape(dec_batch, n_mem, MEM_WIDTH)
        mv_s = cache_mem_v[l].reshape(dec_batch, n_mem, MEM_WIDTH)
        xs8 = _pad_rows(xs)
        if l % 2 == 0:
            w = w_in_sb[i]
            ws = [w[:, :SB_WIDTH], w[:, SB_WIDTH:2 * SB_WIDTH], w[:, 2 * SB_WIDTH:3 * SB_WIDTH], w[:, 3 * SB_WIDTH:]]
            q, k, v, qm = norm_matmul(xp, norm_mix_g[l], ws, 512)
            sbk_p.append(k.reshape(batch, seq, N_SB_HEADS, HEAD_DIM))
            sbv_p.append(v.reshape(batch, seq, N_SB_HEADS, HEAD_DIM))
            o_p = sb_attention_prompt(q, k, v, sb_bias[i], batch, seq)
            qs, ks, vs, qms = norm_matmul(xs8, norm_mix_g[l], ws, 512)
            sbk_s.append(_unpad_rows(ks).reshape(dec_batch, 1, N_SB_HEADS, HEAD_DIM))
            sbv_s.append(_unpad_rows(vs).reshape(dec_batch, 1, N_SB_HEADS, HEAD_DIM))
            o_s = sb_attention_sample(_unpad_rows(qs), _pages_channel_major(cache_sb_k[i]),
                                      _pages_channel_major(cache_sb_v[i]), page_table, sb_bias[i])
            o_s = _pad_rows(o_s)
        else:
            ws_full = jnp.zeros((GM_GROUPS, CHUNK, CHUNK), F32).at[:, rows, cols].set(gm_ws_tril[i])
            o_p, qm = gm_mixer_in(xp, norm_mix_g[l], w_in_gm[i], gm_norm_g[i], ws_full, gm_b[i], CHUNK, 512, False)
            o_s, qms, v_s = gm_mixer_in(xs8, norm_mix_g[l], w_in_gm[i], gm_norm_g[i],
                                        ws_full[:, :SAMPLE_ROWS, :SAMPLE_ROWS], gm_b[i][:, :SAMPLE_ROWS],
                                        SAMPLE_ROWS, 512, True)
            gmv_s.append(_unpad_rows(v_s).reshape(dec_batch, 1, GM_WIDTH))
        xp = mixer_out(xp, o_p, qm, mk_p, mv_p, w_out[l], seq, 512)
        xs = _unpad_rows(mixer_out(xs8, o_s, qms, mk_s, mv_s, w_out[l], SAMPLE_ROWS, SAMPLE_ROWS))
        moe_w = (norm_ffn_g[l], w_router_grp[l], b_router_grp[l], w_router_exp[l], b_router_exp[l],
                 w_gate[l], w_up[l], w_down[l])
        xp = hier_moe_residual(xp, *moe_w, 512, 256)
        xs = _unpad_rows(hier_moe_residual_rows(_pad_rows(xs), *moe_w, SAMPLE_ROWS))

    y_prompt = rmsnorm_rows(xp, final_norm_g, 1024).reshape(batch, seq, d)
    y_sample = rmsnorm_rows(xs, final_norm_g, 32).reshape(dec_batch, 1, d)
    return (y_prompt, y_sample, jnp.stack(sbk_p), jnp.stack(sbv_p), jnp.stack(sbk_s), jnp.stack(sbv_s),
            jnp.stack(memk_p), jnp.stack(memv_p), jnp.stack(gmv_s))
```

```python
import functools

import numpy as np
import jax
import jax.numpy as jnp
from jax import lax
from jax.experimental import pallas as pl
from jax.experimental.pallas import tpu as pltpu

F32 = jnp.float32
BF16 = jnp.bfloat16
I32 = jnp.int32

HEAD_DIM = 64
LANES = 128
SUBLANES = 8
N_SB_HEADS = 12
SB_WIDTH = N_SB_HEADS * HEAD_DIM
N_MEM_HEADS = 4
MEM_WIDTH = N_MEM_HEADS * HEAD_DIM
GM_GROUPS = 4
GM_WIDTH = 768
GM_GROUP_DIM = GM_WIDTH // GM_GROUPS
CHUNK = 128
PAGE_SIZE = 128
N_GROUPS = 4
EXPERTS_PER_GROUP = 8
N_EXPERTS = N_GROUPS * EXPERTS_PER_GROUP
TOP_K = 2
RMS_EPS = 1e-6
ATTN_SCALE = HEAD_DIM ** -0.5

SB_TK = 128
SB_TQ = 256
SB_STEP_HEADS = 12
PAGES_PER_STEP = 16
SAMPLE_ROWS = 2 * SUBLANES
VMEM_LIMIT = 48 * 1024 * 1024
NT_DIMS = (((1,), (1,)), ((), ()))
ISSUE_UNROLL = 8


def _cparams(n_axes):
    return pltpu.CompilerParams(dimension_semantics=("arbitrary",) * n_axes,
                                vmem_limit_bytes=VMEM_LIMIT)


def _mm(a, b):
    return jnp.dot(a.astype(BF16), b.astype(BF16), preferred_element_type=F32)


def _mm_nt(a, b):
    return lax.dot_general(a.astype(BF16), b.astype(BF16), NT_DIMS, preferred_element_type=F32)


def _split_bf16(x):
    hi = x.astype(BF16)
    lo = (x - hi.astype(F32)).astype(BF16)
    return hi, lo


def _round_bf16(x):
    return x.astype(BF16).astype(F32)


def _rms(x, g):
    return x * lax.rsqrt(jnp.mean(x * x, axis=-1, keepdims=True) + RMS_EPS) * g


def _gelu(x):
    return x * (0.5 * (1.0 + jnp.tanh(np.sqrt(2.0 / np.pi).astype(np.float32) * (x + 0.044715 * (x * x * x)))))


def _log_sigmoid_pair(z):
    l = jnp.log(1.0 + jnp.exp(-jnp.abs(z)))
    return jnp.minimum(z, 0.0) - l, -(jnp.maximum(z, 0.0) + l)


def _cumsum_weights():
    j = np.arange(SB_TK)[:, None]
    s = np.arange(SB_TK)[None, :]
    w = np.concatenate([(j > s).astype(np.float32), np.ones((SB_TK, SB_TK), np.float32)], axis=1)
    return jnp.asarray(np.concatenate([w, w], axis=0), dtype=BF16)


def _later_sums(log_keep, w2):
    hi, lo = _split_bf16(log_keep)
    return jnp.dot(jnp.concatenate([hi, lo], axis=1), w2, preferred_element_type=F32)


def _norm_matmul_kernel(x_ref, g_ref, *refs, n_out):
    h = _rms(x_ref[...], g_ref[...]).astype(BF16)
    for w_ref, o_ref in zip(refs[:n_out], refs[n_out:]):
        o_ref[...] = jnp.dot(h, w_ref[...], preferred_element_type=F32)


def norm_matmul(x, g, ws, tm):
    n, d = x.shape
    tm = min(tm, n)
    assert n % tm == 0
    ws = [w.astype(BF16) for w in ws]
    in_specs = [pl.BlockSpec((tm, d), lambda i: (i, 0)), pl.BlockSpec((1, d), lambda i: (0, 0))]
    in_specs += [pl.BlockSpec(w.shape, lambda i: (0, 0)) for w in ws]
    out_specs = [pl.BlockSpec((tm, w.shape[1]), lambda i: (i, 0)) for w in ws]
    out_shape = [jax.ShapeDtypeStruct((n, w.shape[1]), F32) for w in ws]
    return pl.pallas_call(
        functools.partial(_norm_matmul_kernel, n_out=len(ws)),
        grid=(n // tm,), in_specs=in_specs, out_specs=out_specs, out_shape=out_shape,
        compiler_params=_cparams(1))(x, g.reshape(1, d), *ws)


def _rmsnorm_kernel(x_ref, g_ref, o_ref):
    o_ref[...] = _rms(x_ref[...], g_ref[...])


def rmsnorm_rows(x, g, tm):
    n, d = x.shape
    tm = min(tm, n)
    assert n % tm == 0
    return pl.pallas_call(
        _rmsnorm_kernel, grid=(n // tm,),
        in_specs=[pl.BlockSpec((tm, d), lambda i: (i, 0)), pl.BlockSpec((1, d), lambda i: (0, 0))],
        out_specs=pl.BlockSpec((tm, d), lambda i: (i, 0)),
        out_shape=jax.ShapeDtypeStruct((n, d), F32), compiler_params=_cparams(1))(x, g.reshape(1, d))


def _sb_prompt_kernel(bias_ref, q_ref, k_ref, v_ref, w2_ref, o_ref, kb_ref, vb_ref, acc_ref, later_ref):
    hg = pl.program_id(1)
    qi = pl.program_id(2)

    @pl.when(qi == 0)
    def _():
        kb_ref[...] = k_ref[...].astype(BF16)
        vb_ref[...] = v_ref[...].astype(BF16)

    lane = lax.broadcasted_iota(I32, (SB_TQ, LANES), 1)
    half_masks = (lane < HEAD_DIM, lane >= HEAD_DIM)
    q = q_ref[...] * ATTN_SCALE
    blocks = [slice((h // 2) * LANES, (h // 2 + 1) * LANES) for h in range(SB_STEP_HEADS)]
    qh = [jnp.where(half_masks[h % 2], q[:, blocks[h]], 0.0).astype(BF16) for h in range(SB_STEP_HEADS)]
    bias = [bias_ref[SB_STEP_HEADS * hg + h] for h in range(SB_STEP_HEADS)]
    w2 = w2_ref[...]
    row = lax.broadcasted_iota(I32, (SB_TQ, SB_TQ), 0)
    col = lax.broadcasted_iota(I32, (SB_TQ, SB_TQ), 1)
    strictly_before = col < row
    acc_ref[...] = jnp.zeros_like(acc_ref)
    later_ref[...] = jnp.zeros_like(later_ref)

    def key_step(j, diagonal):
        start = pl.multiple_of(j * SB_TQ, SB_TQ)
        for h in range(SB_STEP_HEADS):
            kj = kb_ref[pl.ds(start, SB_TQ), blocks[h]]
            vj = vb_ref[pl.ds(start, SB_TQ), blocks[h]]
            z = lax.dot_general(qh[h], kj, NT_DIMS, preferred_element_type=F32) + bias[h]
            log_beta = jnp.minimum(z, 0.0) - jnp.log(1.0 + jnp.exp(-jnp.abs(z)))
            log_keep = log_beta - z
            if diagonal:
                log_keep = jnp.where(strictly_before, log_keep, 0.0)
            later = later_ref[h]
            parts = [None] * (SB_TQ // SB_TK)
            for s in reversed(range(SB_TQ // SB_TK)):
                cols = slice(s * SB_TK, (s + 1) * SB_TK)
                sums = _later_sums(log_keep[:, cols], w2)
                parts[s] = jnp.exp(log_beta[:, cols] + sums[:, :SB_TK] + later)
                later = later + sums[:, SB_TK:]
            a = jnp.concatenate(parts, axis=1)
            if diagonal:
                a = jnp.where(strictly_before, a, 0.0)
            acc_ref[h] += jnp.dot(a.astype(BF16), vj, preferred_element_type=F32)
            later_ref[h] = later

    key_step(qi, True)

    def body(t, c):
        key_step(qi - 1 - t, False)
        return c
    lax.fori_loop(0, qi, body, 0)
    for b in range(SB_STEP_HEADS // 2):
        o_ref[:, b * LANES:(b + 1) * LANES] = jnp.where(half_masks[0], acc_ref[2 * b], acc_ref[2 * b + 1])


def sb_attention_prompt(q, k, v, bias, batch, seq):
    n = q.shape[0]
    assert seq % SB_TQ == 0 and SB_TQ % SB_TK == 0 and n == batch * seq and N_SB_HEADS % SB_STEP_HEADS == 0
    nq = seq // SB_TQ
    width = SB_STEP_HEADS * HEAD_DIM
    return pl.pallas_call(
        _sb_prompt_kernel,
        grid=(batch, N_SB_HEADS // SB_STEP_HEADS, nq),
        in_specs=[
            pl.BlockSpec(memory_space=pltpu.SMEM),
            pl.BlockSpec((SB_TQ, width), lambda b, hg, qi: (b * nq + qi, hg)),
            pl.BlockSpec((seq, width), lambda b, hg, qi: (b, hg)),
            pl.BlockSpec((seq, width), lambda b, hg, qi: (b, hg)),
            pl.BlockSpec((2 * SB_TK, 2 * SB_TK), lambda b, hg, qi: (0, 0)),
        ],
        out_specs=pl.BlockSpec((SB_TQ, width), lambda b, hg, qi: (b * nq + qi, hg)),
        out_shape=jax.ShapeDtypeStruct((n, SB_WIDTH), F32),
        scratch_shapes=[pltpu.VMEM((seq, width), BF16), pltpu.VMEM((seq, width), BF16),
                        pltpu.VMEM((SB_STEP_HEADS, SB_TQ, LANES), F32),
                        pltpu.VMEM((SB_STEP_HEADS, SB_TQ, LANES), F32)],
        compiler_params=_cparams(3))(bias, q, k, v, _cumsum_weights())


SB_HEAD_ROWS = 16


def _sb_sample_kernel(pt_ref, q_ref, bias_ref, w2_ref, *refs):
    del pt_ref
    k_refs = refs[:PAGES_PER_STEP]
    v_refs = refs[PAGES_PER_STEP:2 * PAGES_PER_STEP]
    o_ref = refs[2 * PAGES_PER_STEP]
    qcol_ref, z_ref, a_ref, acc_ref, later_ref = refs[2 * PAGES_PER_STEP + 1:]
    j = pl.program_id(1)

    @pl.when(j == 0)
    def _():
        acc_ref[...] = jnp.zeros_like(acc_ref)
        later_ref[...] = jnp.zeros_like(later_ref)
        z_ref[...] = jnp.zeros_like(z_ref)
        qcol_ref[...] = _round_bf16(jnp.broadcast_to(q_ref[0] * ATTN_SCALE, qcol_ref.shape))

    heads = [slice(h * HEAD_DIM, (h + 1) * HEAD_DIM) for h in range(N_SB_HEADS)]
    for i in range(PAGES_PER_STEP):
        for h, rows in enumerate(heads):
            prod = _round_bf16(k_refs[i][0, rows, :]) * qcol_ref[rows, :]
            z_ref[pl.ds(i * SB_HEAD_ROWS + h, 1), :] = jnp.sum(prod, axis=0, keepdims=True)
    log_beta, log_keep = _log_sigmoid_pair(z_ref[...] + bias_ref[...])
    sums = _later_sums(log_keep, w2_ref[...])
    later = later_ref[...]
    for i in range(PAGES_PER_STEP):
        r = slice(i * SB_HEAD_ROWS, (i + 1) * SB_HEAD_ROWS)
        a_ref[r, :] = _round_bf16(jnp.exp(log_beta[r] + sums[r, :PAGE_SIZE] + later))
        later = later + sums[r, PAGE_SIZE:]
    later_ref[...] = later
    for h, rows in enumerate(heads):
        acc = acc_ref[rows, :]
        for i in range(PAGES_PER_STEP):
            acc = acc + _round_bf16(v_refs[i][0, rows, :]) * a_ref[pl.ds(i * SB_HEAD_ROWS + h, 1), :]
        acc_ref[rows, :] = acc

    @pl.when(j == pl.num_programs(1) - 1)
    def _():
        o_ref[0] = jnp.sum(acc_ref[...], axis=1, keepdims=True)


def sb_attention_sample(q, cache_kt, cache_vt, page_table, bias):
    nb, n_pages = page_table.shape
    assert n_pages % PAGES_PER_STEP == 0 and PAGE_SIZE == SB_TK
    steps = n_pages // PAGES_PER_STEP
    bias_rows = jnp.zeros((SB_HEAD_ROWS, PAGE_SIZE), F32).at[:N_SB_HEADS].set(
        jnp.broadcast_to(bias[:, None], (N_SB_HEADS, PAGE_SIZE)))
    bias_rows = jnp.tile(bias_rows, (PAGES_PER_STEP, 1))
    step_rows = PAGES_PER_STEP * SB_HEAD_ROWS

    def page_spec(i):
        return pl.BlockSpec((1, SB_WIDTH, PAGE_SIZE),
                            lambda b, j, pt: (pt[b, n_pages - 1 - (j * PAGES_PER_STEP + i)], 0, 0))

    grid_spec = pltpu.PrefetchScalarGridSpec(
        num_scalar_prefetch=1, grid=(nb, steps),
        in_specs=[pl.BlockSpec((1, SB_WIDTH, 1), lambda b, j, pt: (b, 0, 0)),
                  pl.BlockSpec((step_rows, PAGE_SIZE), lambda b, j, pt: (0, 0)),
                  pl.BlockSpec((2 * SB_TK, 2 * SB_TK), lambda b, j, pt: (0, 0))]
        + [page_spec(i) for i in range(PAGES_PER_STEP)] * 2,
        out_specs=pl.BlockSpec((1, SB_WIDTH, 1), lambda b, j, pt: (b, 0, 0)),
        scratch_shapes=[pltpu.VMEM((SB_WIDTH, PAGE_SIZE), F32),
                        pltpu.VMEM((step_rows, PAGE_SIZE), F32),
                        pltpu.VMEM((step_rows, PAGE_SIZE), F32),
                        pltpu.VMEM((SB_WIDTH, PAGE_SIZE), F32),
                        pltpu.VMEM((SB_HEAD_ROWS, PAGE_SIZE), F32)])
    out = pl.pallas_call(
        _sb_sample_kernel, grid_spec=grid_spec,
        out_shape=jax.ShapeDtypeStruct((nb, SB_WIDTH, 1), F32),
        compiler_params=_cparams(2))(
            page_table, q.reshape(nb, SB_WIDTH, 1), bias_rows, _cumsum_weights(),
            *([cache_kt] * PAGES_PER_STEP), *([cache_vt] * PAGES_PER_STEP))
    return out.reshape(nb, SB_WIDTH)


def _pages_channel_major(cache):
    n_phys = cache.shape[0]
    return jnp.transpose(cache, (0, 2, 3, 1)).reshape(n_phys, SB_WIDTH, PAGE_SIZE)


def _gm_kernel(x_ref, g_ref, wu_ref, wv_ref, wq_ref, vg_ref, ws_ref, bias_ref, s_ref, qm_ref, *v_out, chunk):
    h = _rms(x_ref[...], g_ref[...]).astype(BF16)
    qm_ref[...] = jnp.dot(h, wq_ref[...], preferred_element_type=F32)
    u = _gelu(jnp.dot(h, wu_ref[...], preferred_element_type=F32))
    v = _rms(_gelu(jnp.dot(h, wv_ref[...], preferred_element_type=F32)), vg_ref[...])
    if v_out:
        v_out[0][...] = v
    vb = v.astype(BF16)
    group = lax.broadcasted_iota(I32, (chunk, GM_WIDTH), 1) // GM_GROUP_DIM
    for c in range(v.shape[0] // chunk):
        rows = slice(c * chunk, (c + 1) * chunk)
        mixed = bias_ref[...]
        for g in range(GM_GROUPS):
            mixed = mixed + jnp.where(group == g, jnp.dot(ws_ref[g], vb[rows], preferred_element_type=F32), 0.0)
        s_ref[rows, :] = u[rows] * mixed


def gm_mixer_in(x, g, w_in, vg, ws_full, bias, chunk, tm, emit_v):
    n, d = x.shape
    tm = min(tm, n)
    assert n % tm == 0 and tm % chunk == 0
    bias_full = jnp.repeat(bias.T, GM_GROUP_DIM, axis=1)
    w = w_in.astype(BF16)
    ws = [w[:, :GM_WIDTH], w[:, GM_WIDTH:2 * GM_WIDTH], w[:, 2 * GM_WIDTH:]]
    row = lambda i: (i, 0)
    fixed = lambda i: (0, 0)
    widths = [GM_WIDTH, MEM_WIDTH] + ([GM_WIDTH] if emit_v else [])
    return pl.pallas_call(
        functools.partial(_gm_kernel, chunk=chunk), grid=(n // tm,),
        in_specs=[pl.BlockSpec((tm, d), row), pl.BlockSpec((1, d), fixed)]
        + [pl.BlockSpec(wi.shape, fixed) for wi in ws]
        + [pl.BlockSpec((1, GM_WIDTH), fixed), pl.BlockSpec((GM_GROUPS, chunk, chunk), lambda i: (0, 0, 0)),
           pl.BlockSpec((chunk, GM_WIDTH), fixed)],
        out_specs=[pl.BlockSpec((tm, wd), row) for wd in widths],
        out_shape=[jax.ShapeDtypeStruct((n, wd), F32) for wd in widths],
        compiler_params=_cparams(1))(x, g.reshape(1, d), *ws, vg.reshape(1, GM_WIDTH), ws_full.astype(BF16), bias_full)


def _mixer_out_kernel(x_ref, o_ref, qm_ref, mk_ref, mv_ref, wo_ref, out_ref):
    qm = qm_ref[...] * ATTN_SCALE
    mk = mk_ref[0].astype(BF16)
    mv = mv_ref[0].astype(BF16)
    head = lax.broadcasted_iota(I32, qm.shape, 1) // HEAD_DIM
    o_mem = jnp.zeros(qm.shape, F32)
    for h in range(N_MEM_HEADS):
        own = head == h
        s = _mm_nt(jnp.where(own, qm, 0.0), mk)
        p = jnp.exp(s - jnp.max(s, axis=-1, keepdims=True))
        p = p / jnp.sum(p, axis=-1, keepdims=True)
        o_mem = jnp.where(own, _mm(p, mv), o_mem)
    wo = wo_ref[...]
    width = o_ref.shape[1]
    out_ref[...] = x_ref[...] + _mm(o_ref[...], wo[:width]) + _mm(o_mem, wo[width:])


def mixer_out(x, o_mix, qm, mk, mv, wo, rows_per_batch, tm):
    n, d = x.shape
    tm = min(tm, rows_per_batch)
    assert rows_per_batch % tm == 0 and n % tm == 0
    per = rows_per_batch // tm
    wo = wo.astype(BF16)
    row = lambda i: (i, 0)
    mem = lambda i: (i // per, 0, 0)
    return pl.pallas_call(
        _mixer_out_kernel, grid=(n // tm,),
        in_specs=[pl.BlockSpec((tm, d), row), pl.BlockSpec((tm, o_mix.shape[1]), row),
                  pl.BlockSpec((tm, MEM_WIDTH), row),
                  pl.BlockSpec((1,) + mk.shape[1:], mem), pl.BlockSpec((1,) + mv.shape[1:], mem),
                  pl.BlockSpec(wo.shape, lambda i: (0, 0))],
        out_specs=pl.BlockSpec((tm, d), row),
        out_shape=jax.ShapeDtypeStruct((n, d), F32),
        compiler_params=_cparams(1))(x, o_mix, qm, mk, mv, wo)


ROUTE_LANES = LANES
REC = SUBLANES


def _strided(ref, first, count):
    return ref.at[pl.ds(first, count, stride=REC), :]


def _route_kernel(x_ref, g_ref, w_ref, b_ref, tri_ref, rec_ref, re_ref, rw_ref, cnt_ref, seen_ref):
    t = _rms(x_ref[...], g_ref[...])
    tm, d = t.shape
    logits = _mm(t, w_ref[...]) + b_ref[...]
    lane = lax.broadcasted_iota(I32, logits.shape, 1).astype(F32)
    neg = jnp.float32(-jnp.inf)
    far = jnp.float32(ROUTE_LANES)

    def first_max(vals):
        m = jnp.max(vals, axis=-1, keepdims=True)
        return m, jnp.min(jnp.where(vals == m, lane, far), axis=-1, keepdims=True)

    is_group = lane < N_GROUPS
    gmax, gsel = first_max(jnp.where(is_group, logits, neg))
    p_grp = 1.0 / jnp.sum(jnp.where(is_group, jnp.exp(logits - gmax), 0.0), axis=-1, keepdims=True)
    lo = N_GROUPS + gsel * EXPERTS_PER_GROUP
    cand = jnp.where((lane >= lo) & (lane < lo + EXPERTS_PER_GROUP), logits, neg)
    m1, i1 = first_max(cand)
    m2, i2 = first_max(jnp.where(lane == i1, neg, cand))
    e2 = jnp.exp(m2 - m1)
    den = 1.0 + e2
    w1 = (1.0 / den) * p_grp
    w2 = (e2 / den) * p_grp
    ex1 = (i1 - N_GROUPS).astype(I32)
    ex2 = (i2 - N_GROUPS).astype(I32)

    @pl.when(pl.program_id(0) == 0)
    def _():
        seen_ref[...] = jnp.zeros_like(seen_ref)

    pick1 = lane == i1
    pick2 = lane == i2
    chosen = jnp.where(pick1 | pick2, 1.0, 0.0)
    before = jnp.dot(tri_ref[...], chosen.astype(BF16), preferred_element_type=F32) + seen_ref[...]
    rank1 = jnp.sum(jnp.where(pick1, before, 0.0), axis=-1, keepdims=True).astype(I32)
    rank2 = jnp.sum(jnp.where(pick2, before, 0.0), axis=-1, keepdims=True).astype(I32)
    seen_ref[...] += jnp.sum(chosen, axis=0, keepdims=True)
    cnt_ref[...] = seen_ref[...]
    re_ref[...] = jnp.where(lane == 0, ex1, jnp.where(lane == 1, ex2,
                            jnp.where(lane == 2, rank1, jnp.where(lane == 3, rank2, 0))))

    rw_ref[...] = jnp.where(lane == 0, w1, jnp.where(lane == 1, w2, 0.0))
    for j in range(d // LANES):
        _strided(rec_ref, j, tm)[...] = t[:, j * LANES:(j + 1) * LANES]


def moe_route(x, g, wg, bg, wr, br, tm):
    n, d = x.shape
    tm = min(tm, n)
    assert n % tm == 0 and d == REC * LANES
    pad = ROUTE_LANES - N_GROUPS - N_EXPERTS
    w = jnp.concatenate([wg, wr, jnp.zeros((d, pad), F32)], axis=1).astype(BF16)
    b = jnp.concatenate([bg, br, jnp.zeros((pad,), F32)]).reshape(1, ROUTE_LANES)
    earlier_rows = jnp.asarray(np.tril(np.ones((tm, tm), np.float32), -1), dtype=BF16)
    row = lambda i: (i, 0)
    fixed = lambda i: (0, 0)
    return pl.pallas_call(
        _route_kernel, grid=(n // tm,),
        in_specs=[pl.BlockSpec((tm, d), row), pl.BlockSpec((1, d), fixed),
                  pl.BlockSpec((d, ROUTE_LANES), fixed), pl.BlockSpec((1, ROUTE_LANES), fixed),
                  pl.BlockSpec((tm, tm), fixed)],
        out_specs=[pl.BlockSpec((tm * REC, LANES), row), pl.BlockSpec((tm, ROUTE_LANES), row),
                   pl.BlockSpec((tm, ROUTE_LANES), row), pl.BlockSpec((1, ROUTE_LANES), fixed)],
        out_shape=[jax.ShapeDtypeStruct((n * REC, LANES), F32), jax.ShapeDtypeStruct((n, ROUTE_LANES), I32),
                   jax.ShapeDtypeStruct((n, ROUTE_LANES), F32), jax.ShapeDtypeStruct((1, ROUTE_LANES), F32)],
        scratch_shapes=[pltpu.VMEM((1, ROUTE_LANES), F32)],
        compiler_params=_cparams(1))(x, g.reshape(1, d), w, b, earlier_rows)


def _record(ref, index):
    return ref.at[pl.ds(pl.multiple_of(index * REC, REC), REC)]


def _tile_index_specs(nt, width):
    cur = pl.BlockSpec((1, 1, width), lambda i: (i, 0, 0), memory_space=pltpu.SMEM)
    nxt = pl.BlockSpec((1, 1, width), lambda i: (jnp.minimum(i + 1, nt - 1), 0, 0), memory_space=pltpu.SMEM)
    return [cur, nxt]


def _dispatch_kernel(fill_ref, pos_ref, rec_ref, dst_hbm, stage, sem, *, tokens):
    i = pl.program_id(0)
    last = pl.num_programs(0) - 1
    slot = i % 2

    def tile_copy(src, first_record, s):
        return pltpu.make_async_copy(src, dst_hbm.at[pl.ds(pl.multiple_of(first_record * REC, REC), tokens * REC)],
                                     sem.at[s])

    @pl.when(i == 0)
    def _():
        stage[1] = jnp.zeros(stage.shape[1:], stage.dtype)
        for e in range(fill_ref.shape[0]):
            @pl.when(fill_ref[e] >= 0)
            def _():
                tile_copy(stage.at[1], fill_ref[e], 2).start()
        for e in range(fill_ref.shape[0]):
            @pl.when(fill_ref[e] >= 0)
            def _():
                tile_copy(stage.at[1], fill_ref[e], 2).wait()

    def wait_step(s):
        for _ in range(TOP_K):
            tile_copy(stage.at[s], 0, s).wait()

    @pl.when(i >= 2)
    def _():
        wait_step(slot)

    stage[slot] = rec_ref[...]

    def body(r, c):
        src = _record(stage.at[slot], r)
        for k in range(TOP_K):
            pltpu.make_async_copy(src, _record(dst_hbm, pos_ref[0, 0, k * tokens + r]), sem.at[slot]).start()
        return c
    lax.fori_loop(0, tokens, body, 0, unroll=ISSUE_UNROLL)

    @pl.when(i == last)
    def _():
        wait_step(slot)

    @pl.when((i == last) & (i >= 1))
    def _():
        wait_step(1 - slot)


def dispatch_records(rec, pos, fill_first, total, tokens):
    n = pos.shape[0]
    assert n % tokens == 0 and total % tokens == 0
    nt = n // tokens
    idx3 = pos.reshape(nt, tokens, TOP_K).transpose(0, 2, 1).reshape(nt, 1, TOP_K * tokens)
    grid_spec = pltpu.PrefetchScalarGridSpec(
        num_scalar_prefetch=1, grid=(nt,),
        in_specs=[pl.BlockSpec((1, 1, TOP_K * tokens), lambda i, fill: (i, 0, 0), memory_space=pltpu.SMEM),
                  pl.BlockSpec((tokens * REC, LANES), lambda i, fill: (i, 0))],
        out_specs=pl.BlockSpec(memory_space=pl.ANY),
        scratch_shapes=[pltpu.VMEM((2, tokens * REC, LANES), F32), pltpu.SemaphoreType.DMA((3,))])
    return pl.pallas_call(
        functools.partial(_dispatch_kernel, tokens=tokens), grid_spec=grid_spec,
        out_shape=jax.ShapeDtypeStruct((total * REC, LANES), F32),
        compiler_params=_cparams(1))(fill_first, idx3, rec)


def _expert_kernel(te_ref, used_ref, rec_ref, wg_ref, wu_ref, wd_ref, y_ref, wg, wu, wd, *, tm):
    i = pl.program_id(0)
    prev = te_ref[jnp.maximum(i - 1, 0)]

    @pl.when((i == 0) | (te_ref[i] != prev))
    def _():
        wg[...] = wg_ref[0].astype(BF16)
        wu[...] = wu_ref[0].astype(BF16)
        wd[...] = wd_ref[0].astype(BF16)

    @pl.when(i < used_ref[0])
    def _():
        t = jnp.concatenate([_strided(rec_ref, j, tm)[...] for j in range(REC)], axis=1).astype(BF16)
        a = jnp.dot(t, wg[...], preferred_element_type=F32)
        c = jnp.dot(t, wu[...], preferred_element_type=F32)
        y = _mm(a * (1.0 / (1.0 + jnp.exp(-a))) * c, wd[...])
        for j in range(y.shape[1] // LANES):
            _strided(y_ref, j, tm)[...] = y[:, j * LANES:(j + 1) * LANES]

    @pl.when(i >= used_ref[0])
    def _():
        y_ref[...] = jnp.zeros_like(y_ref)


def moe_experts(rec_sorted, tile_expert, n_used, w_gate, w_up, w_down, tm):
    p = rec_sorted.shape[0] // REC
    _, d, f = w_gate.shape
    assert p % tm == 0 and d == REC * LANES and f % LANES == 0
    grid_spec = pltpu.PrefetchScalarGridSpec(
        num_scalar_prefetch=2, grid=(p // tm,),
        in_specs=[pl.BlockSpec((tm * REC, LANES), lambda i, te, nu: (jnp.minimum(i, nu[0] - 1), 0)),
                  pl.BlockSpec((1, d, f), lambda i, te, nu: (te[i], 0, 0)),
                  pl.BlockSpec((1, d, f), lambda i, te, nu: (te[i], 0, 0)),
                  pl.BlockSpec((1, f, d), lambda i, te, nu: (te[i], 0, 0))],
        out_specs=pl.BlockSpec((tm * REC, LANES), lambda i, te, nu: (i, 0)),
        scratch_shapes=[pltpu.VMEM((d, f), BF16), pltpu.VMEM((d, f), BF16), pltpu.VMEM((f, d), BF16)])
    return pl.pallas_call(
        functools.partial(_expert_kernel, tm=tm), grid_spec=grid_spec,
        out_shape=jax.ShapeDtypeStruct((p * REC, LANES), F32),
        compiler_params=_cparams(1))(tile_expert, n_used, rec_sorted, w_gate, w_up, w_down)


def _combine_kernel(idx_ref, nxt_ref, x_ref, rw_ref, y_hbm, out_ref, buf, sem, *, rows):
    i = pl.program_id(0)
    slot = i % 2
    count = TOP_K * rows

    def issue(ref, s):
        def body(r, c):
            pltpu.make_async_copy(_record(y_hbm, ref[0, 0, r]), _record(buf.at[s], r), sem.at[s]).start()
            return c
        lax.fori_loop(0, count, body, 0, unroll=ISSUE_UNROLL)

    @pl.when(i == 0)
    def _():
        issue(idx_ref, 0)

    @pl.when(i + 1 < pl.num_programs(0))
    def _():
        issue(nxt_ref, 1 - slot)

    pltpu.make_async_copy(y_hbm.at[pl.ds(0, count * REC)], buf.at[slot], sem.at[slot]).wait()
    rw = rw_ref[...]
    w1 = jnp.broadcast_to(rw[:, 0:1], (rows, LANES))
    w2 = jnp.broadcast_to(rw[:, 1:2], (rows, LANES))
    for j in range(REC):
        cols = slice(j * LANES, (j + 1) * LANES)
        first = buf[slot, pl.ds(j, rows, stride=REC), :]
        second = buf[slot, pl.ds(rows * REC + j, rows, stride=REC), :]
        out_ref[:, cols] = x_ref[:, cols] + (w1 * first + w2 * second)


def moe_combine(x, y_rec, pos, route_w, rows):
    n, d = x.shape
    rows = min(rows, n)
    assert n % rows == 0 and y_rec.shape[0] >= TOP_K * rows * REC and d == REC * LANES
    nt = n // rows
    idx3 = pos.reshape(nt, rows, TOP_K).transpose(0, 2, 1).reshape(nt, 1, TOP_K * rows)
    return pl.pallas_call(
        functools.partial(_combine_kernel, rows=rows), grid=(nt,),
        in_specs=_tile_index_specs(nt, TOP_K * rows)
        + [pl.BlockSpec((rows, d), lambda i: (i, 0)), pl.BlockSpec((rows, ROUTE_LANES), lambda i: (i, 0)),
           pl.BlockSpec(memory_space=pl.ANY)],
        out_specs=pl.BlockSpec((rows, d), lambda i: (i, 0)),
        out_shape=jax.ShapeDtypeStruct((n, d), F32),
        scratch_shapes=[pltpu.VMEM((2, TOP_K * rows * REC, LANES), F32), pltpu.SemaphoreType.DMA((2,))],
        compiler_params=_cparams(1))(idx3, idx3, x, route_w, y_rec)


def _moe_plan(route_e, seen, tm):
    n = route_e.shape[0]
    experts = jnp.arange(N_EXPERTS, dtype=I32)
    counts = seen[0, N_GROUPS:N_GROUPS + N_EXPERTS].astype(I32)
    padded = ((counts + tm - 1) // tm) * tm
    pad_end = jnp.cumsum(padded)
    pad_start = pad_end - padded
    e = route_e[:, :TOP_K]
    rank = route_e[:, TOP_K:2 * TOP_K]
    pos = jnp.sum(jnp.where(e[:, :, None] == experts, pad_start, 0), axis=-1) + rank
    total = ((TOP_K * n + tm - 1) // tm) * tm + N_EXPERTS * tm
    tile_start = jnp.arange(total // tm, dtype=I32) * tm
    tile_expert = jnp.minimum(jnp.sum((tile_start[:, None] >= pad_end[None, :]).astype(I32), axis=1), N_EXPERTS - 1)
    n_used = (pad_end[-1:] // tm).astype(I32)
    last_tiles = jnp.where(padded > 0, pad_end - tm, -1)
    spare = (total - N_EXPERTS * tm) // tm + jnp.arange(N_EXPERTS, dtype=I32)
    spare_tiles = jnp.where(spare >= n_used[0], spare * tm, -1)
    fill_first = jnp.concatenate([last_tiles, spare_tiles]).astype(I32)
    return pos, tile_expert, n_used, fill_first, total


def hier_moe_residual(x, g, wg, bg, wr, br, w_gate, w_up, w_down, tm_tokens, tm_expert):
    rec, route_e, route_w, seen = moe_route(x, g, wg, bg, wr, br, tm_tokens)
    pos, tile_expert, n_used, fill_first, total = _moe_plan(route_e, seen, tm_expert)
    rec_sorted = dispatch_records(rec, pos, fill_first, total, tm_expert)
    y_rec = moe_experts(rec_sorted, tile_expert, n_used, w_gate, w_up, w_down, tm_expert)
    return moe_combine(x, y_rec, pos, route_w, tm_expert)


def _moe_rows_kernel(e_ref, w_ref, x_ref, g_ref, wg_ref, wu_ref, wd_ref, o_ref):
    a_id = pl.program_id(0) * TOP_K + pl.program_id(1)
    x = x_ref[...]
    t = _rms(x, g_ref[...]).astype(BF16)
    a = jnp.dot(t, wg_ref[0].astype(BF16), preferred_element_type=F32)
    c = jnp.dot(t, wu_ref[0].astype(BF16), preferred_element_type=F32)
    y = _mm(a * (1.0 / (1.0 + jnp.exp(-a))) * c * w_ref[a_id], wd_ref[0])

    @pl.when(pl.program_id(1) == 0)
    def _():
        o_ref[...] = x + y

    @pl.when(pl.program_id(1) > 0)
    def _():
        o_ref[...] += y


def hier_moe_residual_rows(x, g, wg, bg, wr, br, w_gate, w_up, w_down, rows_per_token):
    n, d = x.shape
    nt = n // rows_per_token
    _, f = w_gate.shape[1:]
    _, route_e, route_w, _ = moe_route(x, g, wg, bg, wr, br, n)
    e = route_e.reshape(nt, rows_per_token, ROUTE_LANES)[:, 0, :TOP_K].reshape(-1)
    w = route_w.reshape(nt, rows_per_token, ROUTE_LANES)[:, 0, :TOP_K].reshape(-1)
    expert = lambda i, k, e_ref: (e_ref[i * TOP_K + k], 0, 0)
    grid_spec = pltpu.PrefetchScalarGridSpec(
        num_scalar_prefetch=1, grid=(nt, TOP_K),
        in_specs=[pl.BlockSpec(memory_space=pltpu.SMEM),
                  pl.BlockSpec((rows_per_token, d), lambda i, k, e_ref: (i, 0)),
                  pl.BlockSpec((1, d), lambda i, k, e_ref: (0, 0)),
                  pl.BlockSpec((1, d, f), expert), pl.BlockSpec((1, d, f), expert), pl.BlockSpec((1, f, d), expert)],
        out_specs=pl.BlockSpec((rows_per_token, d), lambda i, k, e_ref: (i, 0)))
    return pl.pallas_call(
        _moe_rows_kernel, grid_spec=grid_spec,
        out_shape=jax.ShapeDtypeStruct((n, d), F32),
        compiler_params=_cparams(2))(e, w, x, g.reshape(1, d), w_gate, w_up, w_down)


def _pad_rows(x):
    b, c = x.shape
    return jnp.zeros((b, SAMPLE_ROWS, c), x.dtype).at[:, 0].set(x).reshape(b * SAMPLE_ROWS, c)


def _unpad_rows(x):
    return x.reshape(-1, SAMPLE_ROWS, x.shape[1])[:, 0]


def kernel(x_prompt, x_sample, mem_prompt, cache_sb_k, cache_sb_v, cache_mem_k, cache_mem_v, page_table, norm_mix_g, norm_ffn_g, norm_mem_g, w_mem_kv, w_in_sb, sb_bias, w_in_gm, gm_norm_g, gm_ws_tril, gm_b, w_out, w_router_grp, b_router_grp, w_router_exp, b_router_exp, w_gate, w_up, w_down, final_norm_g):
    batch, seq, d = x_prompt.shape
    dec_batch, dec_seq, _ = x_sample.shape
    assert dec_seq == 1
    n_mem = mem_prompt.shape[1]
    depth = w_out.shape[0]
    rows, cols = np.tril_indices(CHUNK)

    xp = x_prompt.reshape(batch * seq, d)
    xs = x_sample.reshape(dec_batch, d)
    memp = mem_prompt.reshape(batch * n_mem, d)
    sbk_p, sbv_p, sbk_s, sbv_s, memk_p, memv_p, gmv_s = [], [], [], [], [], [], []

    for l in range(depth):
        i = l // 2
        mk_p, mv_p = norm_matmul(memp, norm_mem_g[l], [w_mem_kv[l][:, :MEM_WIDTH], w_mem_kv[l][:, MEM_WIDTH:]], 512)
        memk_p.append(mk_p.reshape(batch, n_mem, N_MEM_HEADS, HEAD_DIM))
        memv_p.append(mv_p.reshape(batch, n_mem, N_MEM_HEADS, HEAD_DIM))
        mk_p = mk_p.reshape(batch, n_mem, MEM_WIDTH)
        mv_p = mv_p.reshape(batch, n_mem, MEM_WIDTH)
        mk_s = cache_mem_k[l].reshape(dec_batch, n_mem, MEM_WIDTH)
        mv_s = cache_mem_v[l].reshape(dec_batch, n_mem, MEM_WIDTH)
        xs8 = _pad_rows(xs)
        if l % 2 == 0:
            w = w_in_sb[i]
            ws = [w[:, :SB_WIDTH], w[:, SB_WIDTH:2 * SB_WIDTH], w[:, 2 * SB_WIDTH:3 * SB_WIDTH], w[:, 3 * SB_WIDTH:]]
            q, k, v, qm = norm_matmul(xp, norm_mix_g[l], ws, 512)
            sbk_p.append(k.reshape(batch, seq, N_SB_HEADS, HEAD_DIM))
            sbv_p.append(v.reshape(batch, seq, N_SB_HEADS, HEAD_DIM))
            o_p = sb_attention_prompt(q, k, v, sb_bias[i], batch, seq)
            qs, ks, vs, qms = norm_matmul(xs8, norm_mix_g[l], ws, 512)
            sbk_s.append(_unpad_rows(ks).reshape(dec_batch, 1, N_SB_HEADS, HEAD_DIM))
            sbv_s.append(_unpad_rows(vs).reshape(dec_batch, 1, N_SB_HEADS, HEAD_DIM))
            o_s = sb_attention_sample(_unpad_rows(qs), _pages_channel_major(cache_sb_k[i]),
                                      _pages_channel_major(cache_sb_v[i]), page_table, sb_bias[i])
            o_s = _pad_rows(o_s)
        else:
            ws_full = jnp.zeros((GM_GROUPS, CHUNK, CHUNK), F32).at[:, rows, cols].set(gm_ws_tril[i])
            o_p, qm = gm_mixer_in(xp, norm_mix_g[l], w_in_gm[i], gm_norm_g[i], ws_full, gm_b[i], CHUNK, 512, False)
            o_s, qms, v_s = gm_mixer_in(xs8, norm_mix_g[l], w_in_gm[i], gm_norm_g[i],
                                        ws_full[:, :SAMPLE_ROWS, :SAMPLE_ROWS], gm_b[i][:, :SAMPLE_ROWS],
                                        SAMPLE_ROWS, 512, True)
            gmv_s.append(_unpad_rows(v_s).reshape(dec_batch, 1, GM_WIDTH))
        xp = mixer_out(xp, o_p, qm, mk_p, mv_p, w_out[l], seq, 512)
        xs = _unpad_rows(mixer_out(xs8, o_s, qms, mk_s, mv_s, w_out[l], SAMPLE_ROWS, SAMPLE_ROWS))
        moe_w = (norm_ffn_g[l], w_router_grp[l], b_router_grp[l], w_router_exp[l], b_router_exp[l],
                 w_gate[l], w_up[l], w_down[l])
        xp = hier_moe_residual(xp, *moe_w, 512, 256)
        xs = _unpad_rows(hier_moe_residual_rows(_pad_rows(xs), *moe_w, SAMPLE_ROWS))

    y_prompt = rmsnorm_rows(xp, final_norm_g, 1024).reshape(batch, seq, d)
    y_sample = rmsnorm_rows(xs, final_norm_g, 32).reshape(dec_batch, 1, d)
    return (y_prompt, y_sample, jnp.stack(sbk_p), jnp.stack(sbv_p), jnp.stack(sbk_s), jnp.stack(sbv_s),
            jnp.stack(memk_p), jnp.stack(memv_p), jnp.stack(gmv_s))
```

```python
import functools

import numpy as np
import jax
import jax.numpy as jnp
from jax import lax
from jax.experimental import pallas as pl
from jax.experimental.pallas import tpu as pltpu

F32 = jnp.float32
BF16 = jnp.bfloat16
I32 = jnp.int32

HEAD_DIM = 64
LANES = 128
SUBLANES = 8
N_SB_HEADS = 12
SB_WIDTH = N_SB_HEADS * HEAD_DIM
N_MEM_HEADS = 4
MEM_WIDTH = N_MEM_HEADS * HEAD_DIM
GM_GROUPS = 4
GM_WIDTH = 768
GM_GROUP_DIM = GM_WIDTH // GM_GROUPS
CHUNK = 128
PAGE_SIZE = 128
N_GROUPS = 4
EXPERTS_PER_GROUP = 8
N_EXPERTS = N_GROUPS * EXPERTS_PER_GROUP
TOP_K = 2
RMS_EPS = 1e-6
ATTN_SCALE = HEAD_DIM ** -0.5

SB_TK = 128
SB_TQ = 256
SB_STEP_HEADS = 12
PAGES_PER_STEP = 16
SAMPLE_ROWS = 2 * SUBLANES
VMEM_LIMIT = 48 * 1024 * 1024
NT_DIMS = (((1,), (1,)), ((), ()))
ISSUE_UNROLL = 8
DMA_QUEUES = 2


def _cparams(n_axes):
    return pltpu.CompilerParams(dimension_semantics=("arbitrary",) * n_axes,
                                vmem_limit_bytes=VMEM_LIMIT)


def _mm(a, b):
    return jnp.dot(a.astype(BF16), b.astype(BF16), preferred_element_type=F32)


def _mm_nt(a, b):
    return lax.dot_general(a.astype(BF16), b.astype(BF16), NT_DIMS, preferred_element_type=F32)


def _split_bf16(x):
    hi = x.astype(BF16)
    lo = (x - hi.astype(F32)).astype(BF16)
    return hi, lo


def _round_bf16(x):
    return x.astype(BF16).astype(F32)


def _rms(x, g):
    return x * lax.rsqrt(jnp.mean(x * x, axis=-1, keepdims=True) + RMS_EPS) * g


def _gelu(x):
    return x * (0.5 * (1.0 + jnp.tanh(np.sqrt(2.0 / np.pi).astype(np.float32) * (x + 0.044715 * (x * x * x)))))


def _log_sigmoid_pair(z):
    l = jnp.log(1.0 + jnp.exp(-jnp.abs(z)))
    return jnp.minimum(z, 0.0) - l, -(jnp.maximum(z, 0.0) + l)


def _cumsum_weights():
    j = np.arange(SB_TK)[:, None]
    s = np.arange(SB_TK)[None, :]
    w = np.concatenate([(j > s).astype(np.float32), np.ones((SB_TK, SB_TK), np.float32)], axis=1)
    return jnp.asarray(np.concatenate([w, w], axis=0), dtype=BF16)


def _later_sums(log_keep, w2):
    hi, lo = _split_bf16(log_keep)
    return jnp.dot(jnp.concatenate([hi, lo], axis=1), w2, preferred_element_type=F32)


def _norm_matmul_kernel(x_ref, g_ref, *refs, n_out):
    h = _rms(x_ref[...], g_ref[...]).astype(BF16)
    for w_ref, o_ref in zip(refs[:n_out], refs[n_out:]):
        o_ref[...] = jnp.dot(h, w_ref[...], preferred_element_type=F32)


def norm_matmul(x, g, ws, tm):
    n, d = x.shape
    tm = min(tm, n)
    assert n % tm == 0
    ws = [w.astype(BF16) for w in ws]
    in_specs = [pl.BlockSpec((tm, d), lambda i: (i, 0)), pl.BlockSpec((1, d), lambda i: (0, 0))]
    in_specs += [pl.BlockSpec(w.shape, lambda i: (0, 0)) for w in ws]
    out_specs = [pl.BlockSpec((tm, w.shape[1]), lambda i: (i, 0)) for w in ws]
    out_shape = [jax.ShapeDtypeStruct((n, w.shape[1]), F32) for w in ws]
    return pl.pallas_call(
        functools.partial(_norm_matmul_kernel, n_out=len(ws)),
        grid=(n // tm,), in_specs=in_specs, out_specs=out_specs, out_shape=out_shape,
        compiler_params=_cparams(1))(x, g.reshape(1, d), *ws)


def _rmsnorm_kernel(x_ref, g_ref, o_ref):
    o_ref[...] = _rms(x_ref[...], g_ref[...])


def rmsnorm_rows(x, g, tm):
    n, d = x.shape
    tm = min(tm, n)
    assert n % tm == 0
    return pl.pallas_call(
        _rmsnorm_kernel, grid=(n // tm,),
        in_specs=[pl.BlockSpec((tm, d), lambda i: (i, 0)), pl.BlockSpec((1, d), lambda i: (0, 0))],
        out_specs=pl.BlockSpec((tm, d), lambda i: (i, 0)),
        out_shape=jax.ShapeDtypeStruct((n, d), F32), compiler_params=_cparams(1))(x, g.reshape(1, d))


def _sb_prompt_kernel(bias_ref, q_ref, k_ref, v_ref, w2_ref, o_ref, kb_ref, vb_ref, acc_ref, later_ref):
    hg = pl.program_id(1)
    qi = pl.program_id(2)

    @pl.when(qi == 0)
    def _():
        kb_ref[...] = k_ref[...].astype(BF16)
        vb_ref[...] = v_ref[...].astype(BF16)

    lane = lax.broadcasted_iota(I32, (SB_TQ, LANES), 1)
    half_masks = (lane < HEAD_DIM, lane >= HEAD_DIM)
    q = q_ref[...] * ATTN_SCALE
    blocks = [slice((h // 2) * LANES, (h // 2 + 1) * LANES) for h in range(SB_STEP_HEADS)]
    qh = [jnp.where(half_masks[h % 2], q[:, blocks[h]], 0.0).astype(BF16) for h in range(SB_STEP_HEADS)]
    bias = [bias_ref[SB_STEP_HEADS * hg + h] for h in range(SB_STEP_HEADS)]
    w2 = w2_ref[...]
    row = lax.broadcasted_iota(I32, (SB_TQ, SB_TQ), 0)
    col = lax.broadcasted_iota(I32, (SB_TQ, SB_TQ), 1)
    strictly_before = col < row
    acc_ref[...] = jnp.zeros_like(acc_ref)
    later_ref[...] = jnp.zeros_like(later_ref)

    def key_step(j, diagonal):
        start = pl.multiple_of(j * SB_TQ, SB_TQ)
        for h in range(SB_STEP_HEADS):
            kj = kb_ref[pl.ds(start, SB_TQ), blocks[h]]
            vj = vb_ref[pl.ds(start, SB_TQ), blocks[h]]
            z = lax.dot_general(qh[h], kj, NT_DIMS, preferred_element_type=F32) + bias[h]
            log_beta = jnp.minimum(z, 0.0) - jnp.log(1.0 + jnp.exp(-jnp.abs(z)))
            log_keep = log_beta - z
            if diagonal:
                log_keep = jnp.where(strictly_before, log_keep, 0.0)
            later = later_ref[h]
            parts = [None] * (SB_TQ // SB_TK)
            for s in reversed(range(SB_TQ // SB_TK)):
                cols = slice(s * SB_TK, (s + 1) * SB_TK)
                sums = _later_sums(log_keep[:, cols], w2)
                parts[s] = jnp.exp(log_beta[:, cols] + sums[:, :SB_TK] + later)
                later = later + sums[:, SB_TK:]
            a = jnp.concatenate(parts, axis=1)
            if diagonal:
                a = jnp.where(strictly_before, a, 0.0)
            acc_ref[h] += jnp.dot(a.astype(BF16), vj, preferred_element_type=F32)
            later_ref[h] = later

    key_step(qi, True)

    def body(t, c):
        key_step(qi - 1 - t, False)
        return c
    lax.fori_loop(0, qi, body, 0)
    for b in range(SB_STEP_HEADS // 2):
        o_ref[:, b * LANES:(b + 1) * LANES] = jnp.where(half_masks[0], acc_ref[2 * b], acc_ref[2 * b + 1])


def sb_attention_prompt(q, k, v, bias, batch, seq):
    n = q.shape[0]
    assert seq % SB_TQ == 0 and SB_TQ % SB_TK == 0 and n == batch * seq and N_SB_HEADS % SB_STEP_HEADS == 0
    nq = seq // SB_TQ
    width = SB_STEP_HEADS * HEAD_DIM
    return pl.pallas_call(
        _sb_prompt_kernel,
        grid=(batch, N_SB_HEADS // SB_STEP_HEADS, nq),
        in_specs=[
            pl.BlockSpec(memory_space=pltpu.SMEM),
            pl.BlockSpec((SB_TQ, width), lambda b, hg, qi: (b * nq + qi, hg)),
            pl.BlockSpec((seq, width), lambda b, hg, qi: (b, hg)),
            pl.BlockSpec((seq, width), lambda b, hg, qi: (b, hg)),
            pl.BlockSpec((2 * SB_TK, 2 * SB_TK), lambda b, hg, qi: (0, 0)),
        ],
        out_specs=pl.BlockSpec((SB_TQ, width), lambda b, hg, qi: (b * nq + qi, hg)),
        out_shape=jax.ShapeDtypeStruct((n, SB_WIDTH), F32),
        scratch_shapes=[pltpu.VMEM((seq, width), BF16), pltpu.VMEM((seq, width), BF16),
                        pltpu.VMEM((SB_STEP_HEADS, SB_TQ, LANES), F32),
                        pltpu.VMEM((SB_STEP_HEADS, SB_TQ, LANES), F32)],
        compiler_params=_cparams(3))(bias, q, k, v, _cumsum_weights())


SB_HEAD_ROWS = 16


def _sb_sample_kernel(pt_ref, q_ref, bias_ref, w2_ref, *refs):
    del pt_ref
    k_refs = refs[:PAGES_PER_STEP]
    v_refs = refs[PAGES_PER_STEP:2 * PAGES_PER_STEP]
    o_ref = refs[2 * PAGES_PER_STEP]
    qcol_ref, z_ref, a_ref, acc_ref, later_ref = refs[2 * PAGES_PER_STEP + 1:]
    j = pl.program_id(1)

    @pl.when(j == 0)
    def _():
        acc_ref[...] = jnp.zeros_like(acc_ref)
        later_ref[...] = jnp.zeros_like(later_ref)
        z_ref[...] = jnp.zeros_like(z_ref)
        qcol_ref[...] = _round_bf16(jnp.broadcast_to(q_ref[0] * ATTN_SCALE, qcol_ref.shape))

    heads = [slice(h * HEAD_DIM, (h + 1) * HEAD_DIM) for h in range(N_SB_HEADS)]
    for i in range(PAGES_PER_STEP):
        for h, rows in enumerate(heads):
            prod = _round_bf16(k_refs[i][0, rows, :]) * qcol_ref[rows, :]
            z_ref[pl.ds(i * SB_HEAD_ROWS + h, 1), :] = jnp.sum(prod, axis=0, keepdims=True)
    log_beta, log_keep = _log_sigmoid_pair(z_ref[...] + bias_ref[...])
    sums = _later_sums(log_keep, w2_ref[...])
    later = later_ref[...]
    for i in range(PAGES_PER_STEP):
        r = slice(i * SB_HEAD_ROWS, (i + 1) * SB_HEAD_ROWS)
        a_ref[r, :] = _round_bf16(jnp.exp(log_beta[r] + sums[r, :PAGE_SIZE] + later))
        later = later + sums[r, PAGE_SIZE:]
    later_ref[...] = later
    for h, rows in enumerate(heads):
        acc = acc_ref[rows, :]
        for i in range(PAGES_PER_STEP):
            acc = acc + _round_bf16(v_refs[i][0, rows, :]) * a_ref[pl.ds(i * SB_HEAD_ROWS + h, 1), :]
        acc_ref[rows, :] = acc

    @pl.when(j == pl.num_programs(1) - 1)
    def _():
        o_ref[0] = jnp.sum(acc_ref[...], axis=1, keepdims=True)


def sb_attention_sample(q, cache_kt, cache_vt, page_table, bias):
    nb, n_pages = page_table.shape
    assert n_pages % PAGES_PER_STEP == 0 and PAGE_SIZE == SB_TK
    steps = n_pages // PAGES_PER_STEP
    bias_rows = jnp.zeros((SB_HEAD_ROWS, PAGE_SIZE), F32).at[:N_SB_HEADS].set(
        jnp.broadcast_to(bias[:, None], (N_SB_HEADS, PAGE_SIZE)))
    bias_rows = jnp.tile(bias_rows, (PAGES_PER_STEP, 1))
    step_rows = PAGES_PER_STEP * SB_HEAD_ROWS

    def page_spec(i):
        return pl.BlockSpec((1, SB_WIDTH, PAGE_SIZE),
                            lambda b, j, pt: (pt[b, n_pages - 1 - (j * PAGES_PER_STEP + i)], 0, 0))

    grid_spec = pltpu.PrefetchScalarGridSpec(
        num_scalar_prefetch=1, grid=(nb, steps),
        in_specs=[pl.BlockSpec((1, SB_WIDTH, 1), lambda b, j, pt: (b, 0, 0)),
                  pl.BlockSpec((step_rows, PAGE_SIZE), lambda b, j, pt: (0, 0)),
                  pl.BlockSpec((2 * SB_TK, 2 * SB_TK), lambda b, j, pt: (0, 0))]
        + [page_spec(i) for i in range(PAGES_PER_STEP)] * 2,
        out_specs=pl.BlockSpec((1, SB_WIDTH, 1), lambda b, j, pt: (b, 0, 0)),
        scratch_shapes=[pltpu.VMEM((SB_WIDTH, PAGE_SIZE), F32),
                        pltpu.VMEM((step_rows, PAGE_SIZE), F32),
                        pltpu.VMEM((step_rows, PAGE_SIZE), F32),
                        pltpu.VMEM((SB_WIDTH, PAGE_SIZE), F32),
                        pltpu.VMEM((SB_HEAD_ROWS, PAGE_SIZE), F32)])
    out = pl.pallas_call(
        _sb_sample_kernel, grid_spec=grid_spec,
        out_shape=jax.ShapeDtypeStruct((nb, SB_WIDTH, 1), F32),
        compiler_params=_cparams(2))(
            page_table, q.reshape(nb, SB_WIDTH, 1), bias_rows, _cumsum_weights(),
            *([cache_kt] * PAGES_PER_STEP), *([cache_vt] * PAGES_PER_STEP))
    return out.reshape(nb, SB_WIDTH)


def _pages_channel_major(cache):
    n_phys = cache.shape[0]
    return jnp.transpose(cache, (0, 2, 3, 1)).reshape(n_phys, SB_WIDTH, PAGE_SIZE)


def _gm_kernel(x_ref, g_ref, wu_ref, wv_ref, wq_ref, vg_ref, ws_ref, bias_ref, s_ref, qm_ref, *v_out, chunk):
    h = _rms(x_ref[...], g_ref[...]).astype(BF16)
    qm_ref[...] = jnp.dot(h, wq_ref[...], preferred_element_type=F32)
    u = _gelu(jnp.dot(h, wu_ref[...], preferred_element_type=F32))
    v = _rms(_gelu(jnp.dot(h, wv_ref[...], preferred_element_type=F32)), vg_ref[...])
    if v_out:
        v_out[0][...] = v
    vb = v.astype(BF16)
    group = lax.broadcasted_iota(I32, (chunk, GM_WIDTH), 1) // GM_GROUP_DIM
    for c in range(v.shape[0] // chunk):
        rows = slice(c * chunk, (c + 1) * chunk)
        mixed = bias_ref[...]
        for g in range(GM_GROUPS):
            mixed = mixed + jnp.where(group == g, jnp.dot(ws_ref[g], vb[rows], preferred_element_type=F32), 0.0)
        s_ref[rows, :] = u[rows] * mixed


def gm_mixer_in(x, g, w_in, vg, ws_full, bias, chunk, tm, emit_v):
    n, d = x.shape
    tm = min(tm, n)
    assert n % tm == 0 and tm % chunk == 0
    bias_full = jnp.repeat(bias.T, GM_GROUP_DIM, axis=1)
    w = w_in.astype(BF16)
    ws = [w[:, :GM_WIDTH], w[:, GM_WIDTH:2 * GM_WIDTH], w[:, 2 * GM_WIDTH:]]
    row = lambda i: (i, 0)
    fixed = lambda i: (0, 0)
    widths = [GM_WIDTH, MEM_WIDTH] + ([GM_WIDTH] if emit_v else [])
    return pl.pallas_call(
        functools.partial(_gm_kernel, chunk=chunk), grid=(n // tm,),
        in_specs=[pl.BlockSpec((tm, d), row), pl.BlockSpec((1, d), fixed)]
        + [pl.BlockSpec(wi.shape, fixed) for wi in ws]
        + [pl.BlockSpec((1, GM_WIDTH), fixed), pl.BlockSpec((GM_GROUPS, chunk, chunk), lambda i: (0, 0, 0)),
           pl.BlockSpec((chunk, GM_WIDTH), fixed)],
        out_specs=[pl.BlockSpec((tm, wd), row) for wd in widths],
        out_shape=[jax.ShapeDtypeStruct((n, wd), F32) for wd in widths],
        compiler_params=_cparams(1))(x, g.reshape(1, d), *ws, vg.reshape(1, GM_WIDTH), ws_full.astype(BF16), bias_full)


def _mixer_out_kernel(x_ref, o_ref, qm_ref, mk_ref, mv_ref, wo_ref, out_ref):
    qm = qm_ref[...] * ATTN_SCALE
    mk = mk_ref[0].astype(BF16)
    mv = mv_ref[0].astype(BF16)
    head = lax.broadcasted_iota(I32, qm.shape, 1) // HEAD_DIM
    o_mem = jnp.zeros(qm.shape, F32)
    for h in range(N_MEM_HEADS):
        own = head == h
        s = _mm_nt(jnp.where(own, qm, 0.0), mk)
        p = jnp.exp(s - jnp.max(s, axis=-1, keepdims=True))
        p = p / jnp.sum(p, axis=-1, keepdims=True)
        o_mem = jnp.where(own, _mm(p, mv), o_mem)
    wo = wo_ref[...]
    width = o_ref.shape[1]
    out_ref[...] = x_ref[...] + _mm(o_ref[...], wo[:width]) + _mm(o_mem, wo[width:])


def mixer_out(x, o_mix, qm, mk, mv, wo, rows_per_batch, tm):
    n, d = x.shape
    tm = min(tm, rows_per_batch)
    assert rows_per_batch % tm == 0 and n % tm == 0
    per = rows_per_batch // tm
    wo = wo.astype(BF16)
    row = lambda i: (i, 0)
    mem = lambda i: (i // per, 0, 0)
    return pl.pallas_call(
        _mixer_out_kernel, grid=(n // tm,),
        in_specs=[pl.BlockSpec((tm, d), row), pl.BlockSpec((tm, o_mix.shape[1]), row),
                  pl.BlockSpec((tm, MEM_WIDTH), row),
                  pl.BlockSpec((1,) + mk.shape[1:], mem), pl.BlockSpec((1,) + mv.shape[1:], mem),
                  pl.BlockSpec(wo.shape, lambda i: (0, 0))],
        out_specs=pl.BlockSpec((tm, d), row),
        out_shape=jax.ShapeDtypeStruct((n, d), F32),
        compiler_params=_cparams(1))(x, o_mix, qm, mk, mv, wo)


ROUTE_LANES = LANES
REC = SUBLANES


def _strided(ref, first, count):
    return ref.at[pl.ds(first, count, stride=REC), :]


def _route_kernel(x_ref, g_ref, w_ref, b_ref, tri_ref, rec_ref, re_ref, rw_ref, cnt_ref, seen_ref):
    t = _rms(x_ref[...], g_ref[...])
    tm, d = t.shape
    logits = _mm(t, w_ref[...]) + b_ref[...]
    lane = lax.broadcasted_iota(I32, logits.shape, 1).astype(F32)
    neg = jnp.float32(-jnp.inf)
    far = jnp.float32(ROUTE_LANES)

    def first_max(vals):
        m = jnp.max(vals, axis=-1, keepdims=True)
        return m, jnp.min(jnp.where(vals == m, lane, far), axis=-1, keepdims=True)

    is_group = lane < N_GROUPS
    gmax, gsel = first_max(jnp.where(is_group, logits, neg))
    p_grp = 1.0 / jnp.sum(jnp.where(is_group, jnp.exp(logits - gmax), 0.0), axis=-1, keepdims=True)
    lo = N_GROUPS + gsel * EXPERTS_PER_GROUP
    cand = jnp.where((lane >= lo) & (lane < lo + EXPERTS_PER_GROUP), logits, neg)
    m1, i1 = first_max(cand)
    m2, i2 = first_max(jnp.where(lane == i1, neg, cand))
    e2 = jnp.exp(m2 - m1)
    den = 1.0 + e2
    w1 = (1.0 / den) * p_grp
    w2 = (e2 / den) * p_grp
    ex1 = (i1 - N_GROUPS).astype(I32)
    ex2 = (i2 - N_GROUPS).astype(I32)

    @pl.when(pl.program_id(0) == 0)
    def _():
        seen_ref[...] = jnp.zeros_like(seen_ref)

    pick1 = lane == i1
    pick2 = lane == i2
    chosen = jnp.where(pick1 | pick2, 1.0, 0.0)
    before = jnp.dot(tri_ref[...], chosen.astype(BF16), preferred_element_type=F32) + seen_ref[...]
    rank1 = jnp.sum(jnp.where(pick1, before, 0.0), axis=-1, keepdims=True).astype(I32)
    rank2 = jnp.sum(jnp.where(pick2, before, 0.0), axis=-1, keepdims=True).astype(I32)
    seen_ref[...] += jnp.sum(chosen, axis=0, keepdims=True)
    cnt_ref[...] = seen_ref[...]
    re_ref[...] = jnp.where(lane == 0, ex1, jnp.where(lane == 1, ex2,
                            jnp.where(lane == 2, rank1, jnp.where(lane == 3, rank2, 0))))

    rw_ref[...] = jnp.where(lane == 0, w1, jnp.where(lane == 1, w2, 0.0))
    for j in range(d // LANES):
        _strided(rec_ref, j, tm)[...] = t[:, j * LANES:(j + 1) * LANES]


def moe_route(x, g, wg, bg, wr, br, tm):
    n, d = x.shape
    tm = min(tm, n)
    assert n % tm == 0 and d == REC * LANES
    pad = ROUTE_LANES - N_GROUPS - N_EXPERTS
    w = jnp.concatenate([wg, wr, jnp.zeros((d, pad), F32)], axis=1).astype(BF16)
    b = jnp.concatenate([bg, br, jnp.zeros((pad,), F32)]).reshape(1, ROUTE_LANES)
    earlier_rows = jnp.asarray(np.tril(np.ones((tm, tm), np.float32), -1), dtype=BF16)
    row = lambda i: (i, 0)
    fixed = lambda i: (0, 0)
    return pl.pallas_call(
        _route_kernel, grid=(n // tm,),
        in_specs=[pl.BlockSpec((tm, d), row), pl.BlockSpec((1, d), fixed),
                  pl.BlockSpec((d, ROUTE_LANES), fixed), pl.BlockSpec((1, ROUTE_LANES), fixed),
                  pl.BlockSpec((tm, tm), fixed)],
        out_specs=[pl.BlockSpec((tm * REC, LANES), row), pl.BlockSpec((tm, ROUTE_LANES), row),
                   pl.BlockSpec((tm, ROUTE_LANES), row), pl.BlockSpec((1, ROUTE_LANES), fixed)],
        out_shape=[jax.ShapeDtypeStruct((n * REC, LANES), F32), jax.ShapeDtypeStruct((n, ROUTE_LANES), I32),
                   jax.ShapeDtypeStruct((n, ROUTE_LANES), F32), jax.ShapeDtypeStruct((1, ROUTE_LANES), F32)],
        scratch_shapes=[pltpu.VMEM((1, ROUTE_LANES), F32)],
        compiler_params=_cparams(1))(x, g.reshape(1, d), w, b, earlier_rows)


def _record(ref, index):
    return ref.at[pl.ds(pl.multiple_of(index * REC, REC), REC)]


def _tile_index_specs(nt, width):
    cur = pl.BlockSpec((1, 1, width), lambda i: (i, 0, 0), memory_space=pltpu.SMEM)
    nxt = pl.BlockSpec((1, 1, width), lambda i: (jnp.minimum(i + 1, nt - 1), 0, 0), memory_space=pltpu.SMEM)
    return [cur, nxt]


def _dispatch_kernel(fill_ref, pos_ref, rec_ref, dst_hbm, stage, sem, *, tokens):
    i = pl.program_id(0)
    last = pl.num_programs(0) - 1
    slot = i % 2

    def tile_copy(src, first_record, s):
        return pltpu.make_async_copy(src, dst_hbm.at[pl.ds(pl.multiple_of(first_record * REC, REC), tokens * REC)],
                                     sem.at[s])

    @pl.when(i == 0)
    def _():
        stage[1] = jnp.zeros(stage.shape[1:], stage.dtype)
        for e in range(fill_ref.shape[0]):
            @pl.when(fill_ref[e] >= 0)
            def _():
                tile_copy(stage.at[1], fill_ref[e], 2).start()
        for e in range(fill_ref.shape[0]):
            @pl.when(fill_ref[e] >= 0)
            def _():
                tile_copy(stage.at[1], fill_ref[e], 2).wait()

    def wait_step(s):
        for _ in range(TOP_K):
            tile_copy(stage.at[s], 0, s).wait()

    @pl.when(i >= 2)
    def _():
        wait_step(slot)

    stage[slot] = rec_ref[...]

    def body(r, c):
        src = _record(stage.at[slot], r)
        for k in range(TOP_K):
            pltpu.make_async_copy(src, _record(dst_hbm, pos_ref[0, 0, k * tokens + r]),
                                  sem.at[slot]).start(priority=k % DMA_QUEUES)
        return c
    lax.fori_loop(0, tokens, body, 0, unroll=ISSUE_UNROLL)

    @pl.when(i == last)
    def _():
        wait_step(slot)

    @pl.when((i == last) & (i >= 1))
    def _():
        wait_step(1 - slot)


def dispatch_records(rec, pos, fill_first, total, tokens):
    n = pos.shape[0]
    assert n % tokens == 0 and total % tokens == 0
    nt = n // tokens
    idx3 = pos.reshape(nt, tokens, TOP_K).transpose(0, 2, 1).reshape(nt, 1, TOP_K * tokens)
    grid_spec = pltpu.PrefetchScalarGridSpec(
        num_scalar_prefetch=1, grid=(nt,),
        in_specs=[pl.BlockSpec((1, 1, TOP_K * tokens), lambda i, fill: (i, 0, 0), memory_space=pltpu.SMEM),
                  pl.BlockSpec((tokens * REC, LANES), lambda i, fill: (i, 0))],
        out_specs=pl.BlockSpec(memory_space=pl.ANY),
        scratch_shapes=[pltpu.VMEM((2, tokens * REC, LANES), F32), pltpu.SemaphoreType.DMA((3,))])
    return pl.pallas_call(
        functools.partial(_dispatch_kernel, tokens=tokens), grid_spec=grid_spec,
        out_shape=jax.ShapeDtypeStruct((total * REC, LANES), F32),
        compiler_params=_cparams(1))(fill_first, idx3, rec)


def _expert_kernel(te_ref, used_ref, rec_ref, wg_ref, wu_ref, wd_ref, y_ref, wg, wu, wd, *, tm):
    i = pl.program_id(0)
    prev = te_ref[jnp.maximum(i - 1, 0)]

    @pl.when((i == 0) | (te_ref[i] != prev))
    def _():
        wg[...] = wg_ref[0].astype(BF16)
        wu[...] = wu_ref[0].astype(BF16)
        wd[...] = wd_ref[0].astype(BF16)

    @pl.when(i < used_ref[0])
    def _():
        t = jnp.concatenate([_strided(rec_ref, j, tm)[...] for j in range(REC)], axis=1).astype(BF16)
        a = jnp.dot(t, wg[...], preferred_element_type=F32)
        c = jnp.dot(t, wu[...], preferred_element_type=F32)
        y = _mm(a * (1.0 / (1.0 + jnp.exp(-a))) * c, wd[...])
        for j in range(y.shape[1] // LANES):
            _strided(y_ref, j, tm)[...] = y[:, j * LANES:(j + 1) * LANES]

    @pl.when(i >= used_ref[0])
    def _():
        y_ref[...] = jnp.zeros_like(y_ref)


def moe_experts(rec_sorted, tile_expert, n_used, w_gate, w_up, w_down, tm):
    p = rec_sorted.shape[0] // REC
    _, d, f = w_gate.shape
    assert p % tm == 0 and d == REC * LANES and f % LANES == 0
    grid_spec = pltpu.PrefetchScalarGridSpec(
        num_scalar_prefetch=2, grid=(p // tm,),
        in_specs=[pl.BlockSpec((tm * REC, LANES), lambda i, te, nu: (jnp.minimum(i, nu[0] - 1), 0)),
                  pl.BlockSpec((1, d, f), lambda i, te, nu: (te[i], 0, 0)),
                  pl.BlockSpec((1, d, f), lambda i, te, nu: (te[i], 0, 0)),
                  pl.BlockSpec((1, f, d), lambda i, te, nu: (te[i], 0, 0))],
        out_specs=pl.BlockSpec((tm * REC, LANES), lambda i, te, nu: (i, 0)),
        scratch_shapes=[pltpu.VMEM((d, f), BF16), pltpu.VMEM((d, f), BF16), pltpu.VMEM((f, d), BF16)])
    return pl.pallas_call(
        functools.partial(_expert_kernel, tm=tm), grid_spec=grid_spec,
        out_shape=jax.ShapeDtypeStruct((p * REC, LANES), F32),
        compiler_params=_cparams(1))(tile_expert, n_used, rec_sorted, w_gate, w_up, w_down)


def _combine_kernel(idx_ref, nxt_ref, x_ref, rw_ref, y_hbm, out_ref, buf, sem, *, rows):
    i = pl.program_id(0)
    slot = i % 2
    count = TOP_K * rows

    def issue(ref, s):
        def body(p, c):
            for q in range(DMA_QUEUES):
                r = p * DMA_QUEUES + q
                pltpu.make_async_copy(_record(y_hbm, ref[0, 0, r]), _record(buf.at[s], r),
                                      sem.at[s]).start(priority=q)
            return c
        lax.fori_loop(0, count // DMA_QUEUES, body, 0, unroll=ISSUE_UNROLL)

    @pl.when(i == 0)
    def _():
        issue(idx_ref, 0)

    @pl.when(i + 1 < pl.num_programs(0))
    def _():
        issue(nxt_ref, 1 - slot)

    pltpu.make_async_copy(y_hbm.at[pl.ds(0, count * REC)], buf.at[slot], sem.at[slot]).wait()
    rw = rw_ref[...]
    w1 = jnp.broadcast_to(rw[:, 0:1], (rows, LANES))
    w2 = jnp.broadcast_to(rw[:, 1:2], (rows, LANES))
    for j in range(REC):
        cols = slice(j * LANES, (j + 1) * LANES)
        first = buf[slot, pl.ds(j, rows, stride=REC), :]
        second = buf[slot, pl.ds(rows * REC + j, rows, stride=REC), :]
        out_ref[:, cols] = x_ref[:, cols] + (w1 * first + w2 * second)


def moe_combine(x, y_rec, pos, route_w, rows):
    n, d = x.shape
    rows = min(rows, n)
    assert n % rows == 0 and y_rec.shape[0] >= TOP_K * rows * REC and d == REC * LANES
    nt = n // rows
    idx3 = pos.reshape(nt, rows, TOP_K).transpose(0, 2, 1).reshape(nt, 1, TOP_K * rows)
    return pl.pallas_call(
        functools.partial(_combine_kernel, rows=rows), grid=(nt,),
        in_specs=_tile_index_specs(nt, TOP_K * rows)
        + [pl.BlockSpec((rows, d), lambda i: (i, 0)), pl.BlockSpec((rows, ROUTE_LANES), lambda i: (i, 0)),
           pl.BlockSpec(memory_space=pl.ANY)],
        out_specs=pl.BlockSpec((rows, d), lambda i: (i, 0)),
        out_shape=jax.ShapeDtypeStruct((n, d), F32),
        scratch_shapes=[pltpu.VMEM((2, TOP_K * rows * REC, LANES), F32), pltpu.SemaphoreType.DMA((2,))],
        compiler_params=_cparams(1))(idx3, idx3, x, route_w, y_rec)


def _moe_plan(route_e, seen, tm):
    n = route_e.shape[0]
    experts = jnp.arange(N_EXPERTS, dtype=I32)
    counts = seen[0, N_GROUPS:N_GROUPS + N_EXPERTS].astype(I32)
    padded = ((counts + tm - 1) // tm) * tm
    pad_end = jnp.cumsum(padded)
    pad_start = pad_end - padded
    e = route_e[:, :TOP_K]
    rank = route_e[:, TOP_K:2 * TOP_K]
    pos = jnp.sum(jnp.where(e[:, :, None] == experts, pad_start, 0), axis=-1) + rank
    total = ((TOP_K * n + tm - 1) // tm) * tm + N_EXPERTS * tm
    tile_start = jnp.arange(total // tm, dtype=I32) * tm
    tile_expert = jnp.minimum(jnp.sum((tile_start[:, None] >= pad_end[None, :]).astype(I32), axis=1), N_EXPERTS - 1)
    n_used = (pad_end[-1:] // tm).astype(I32)
    last_tiles = jnp.where(padded > 0, pad_end - tm, -1)
    spare = (total - N_EXPERTS * tm) // tm + jnp.arange(N_EXPERTS, dtype=I32)
    spare_tiles = jnp.where(spare >= n_used[0], spare * tm, -1)
    fill_first = jnp.concatenate([last_tiles, spare_tiles]).astype(I32)
    return pos, tile_expert, n_used, fill_first, total


def hier_moe_residual(x, g, wg, bg, wr, br, w_gate, w_up, w_down, tm_tokens, tm_expert):
    rec, route_e, route_w, seen = moe_route(x, g, wg, bg, wr, br, tm_tokens)
    pos, tile_expert, n_used, fill_first, total = _moe_plan(route_e, seen, tm_expert)
    rec_sorted = dispatch_records(rec, pos, fill_first, total, tm_expert)
    y_rec = moe_experts(rec_sorted, tile_expert, n_used, w_gate, w_up, w_down, tm_expert)
    return moe_combine(x, y_rec, pos, route_w, tm_expert)


def _moe_rows_kernel(e_ref, w_ref, x_ref, g_ref, wg_ref, wu_ref, wd_ref, o_ref):
    a_id = pl.program_id(0) * TOP_K + pl.program_id(1)
    x = x_ref[...]
    t = _rms(x, g_ref[...]).astype(BF16)
    a = jnp.dot(t, wg_ref[0].astype(BF16), preferred_element_type=F32)
    c = jnp.dot(t, wu_ref[0].astype(BF16), preferred_element_type=F32)
    y = _mm(a * (1.0 / (1.0 + jnp.exp(-a))) * c * w_ref[a_id], wd_ref[0])

    @pl.when(pl.program_id(1) == 0)
    def _():
        o_ref[...] = x + y

    @pl.when(pl.program_id(1) > 0)
    def _():
        o_ref[...] += y


def hier_moe_residual_rows(x, g, wg, bg, wr, br, w_gate, w_up, w_down, rows_per_token):
    n, d = x.shape
    nt = n // rows_per_token
    _, f = w_gate.shape[1:]
    _, route_e, route_w, _ = moe_route(x, g, wg, bg, wr, br, n)
    e = route_e.reshape(nt, rows_per_token, ROUTE_LANES)[:, 0, :TOP_K].reshape(-1)
    w = route_w.reshape(nt, rows_per_token, ROUTE_LANES)[:, 0, :TOP_K].reshape(-1)
    expert = lambda i, k, e_ref: (e_ref[i * TOP_K + k], 0, 0)
    grid_spec = pltpu.PrefetchScalarGridSpec(
        num_scalar_prefetch=1, grid=(nt, TOP_K),
        in_specs=[pl.BlockSpec(memory_space=pltpu.SMEM),
                  pl.BlockSpec((rows_per_token, d), lambda i, k, e_ref: (i, 0)),
                  pl.BlockSpec((1, d), lambda i, k, e_ref: (0, 0)),
                  pl.BlockSpec((1, d, f), expert), pl.BlockSpec((1, d, f), expert), pl.BlockSpec((1, f, d), expert)],
        out_specs=pl.BlockSpec((rows_per_token, d), lambda i, k, e_ref: (i, 0)))
    return pl.pallas_call(
        _moe_rows_kernel, grid_spec=grid_spec,
        out_shape=jax.ShapeDtypeStruct((n, d), F32),
        compiler_params=_cparams(2))(e, w, x, g.reshape(1, d), w_gate, w_up, w_down)


def _pad_rows(x):
    b, c = x.shape
    return jnp.zeros((b, SAMPLE_ROWS, c), x.dtype).at[:, 0].set(x).reshape(b * SAMPLE_ROWS, c)


def _unpad_rows(x):
    return x.reshape(-1, SAMPLE_ROWS, x.shape[1])[:, 0]


def kernel(x_prompt, x_sample, mem_prompt, cache_sb_k, cache_sb_v, cache_mem_k, cache_mem_v, page_table, norm_mix_g, norm_ffn_g, norm_mem_g, w_mem_kv, w_in_sb, sb_bias, w_in_gm, gm_norm_g, gm_ws_tril, gm_b, w_out, w_router_grp, b_router_grp, w_router_exp, b_router_exp, w_gate, w_up, w_down, final_norm_g):
    batch, seq, d = x_prompt.shape
    dec_batch, dec_seq, _ = x_sample.shape
    assert dec_seq == 1
    n_mem = mem_prompt.shape[1]
    depth = w_out.shape[0]
    rows, cols = np.tril_indices(CHUNK)

    xp = x_prompt.reshape(batch * seq, d)
    xs = x_sample.reshape(dec_batch, d)
    memp = mem_prompt.reshape(batch * n_mem, d)
    sbk_p, sbv_p, sbk_s, sbv_s, memk_p, memv_p, gmv_s = [], [], [], [], [], [], []

    for l in range(depth):
        i = l // 2
        mk_p, mv_p = norm_matmul(memp, norm_mem_g[l], [w_mem_kv[l][:, :MEM_WIDTH], w_mem_kv[l][:, MEM_WIDTH:]], 512)
        memk_p.append(mk_p.reshape(batch, n_mem, N_MEM_HEADS, HEAD_DIM))
        memv_p.append(mv_p.reshape(batch, n_mem, N_MEM_HEADS, HEAD_DIM))
        mk_p = mk_p.reshape(batch, n_mem, MEM_WIDTH)
        mv_p = mv_p.reshape(batch, n_mem, MEM_WIDTH)
        mk_s = cache_mem_k[l].reshape(dec_batch, n_mem, MEM_WIDTH)
        mv_s = cache_mem_v[l].reshape(dec_batch, n_mem, MEM_WIDTH)
        xs8 = _pad_rows(xs)
        if l % 2 == 0:
            w = w_in_sb[i]
            ws = [w[:, :SB_WIDTH], w[:, SB_WIDTH:2 * SB_WIDTH], w[:, 2 * SB_WIDTH:3 * SB_WIDTH], w[:, 3 * SB_WIDTH:]]
            q, k, v, qm = norm_matmul(xp, norm_mix_g[l], ws, 512)
            sbk_p.append(k.reshape(batch, seq, N_SB_HEADS, HEAD_DIM))
            sbv_p.append(v.reshape(batch, seq, N_SB_HEADS, HEAD_DIM))
            o_p = sb_attention_prompt(q, k, v, sb_bias[i], batch, seq)
            qs, ks, vs, qms = norm_matmul(xs8, norm_mix_g[l], ws, 512)
            sbk_s.append(_unpad_rows(ks).reshape(dec_batch, 1, N_SB_HEADS, HEAD_DIM))
            sbv_s.append(_unpad_rows(vs).reshape(dec_batch, 1, N_SB_HEADS, HEAD_DIM))
            o_s = sb_attention_sample(_unpad_rows(qs), _pages_channel_major(cache_sb_k[i]),
                                      _pages_channel_major(cache_sb_v[i]), page_table, sb_bias[i])
            o_s = _pad_rows(o_s)
        else:
            ws_full = jnp.zeros((GM_GROUPS, CHUNK, CHUNK), F32).at[:, rows, cols].set(gm_ws_tril[i])
            o_p, qm = gm_mixer_in(xp, norm_mix_g[l], w_in_gm[i], gm_norm_g[i], ws_full, gm_b[i], CHUNK, 512, False)
            o_s, qms, v_s = gm_mixer_in(xs8, norm_mix_g[l], w_in_gm[i], gm_norm_g[i],
                                        ws_full[:, :SAMPLE_ROWS, :SAMPLE_ROWS], gm_b[i][:, :SAMPLE_ROWS],
                                        SAMPLE_ROWS, 512, True)
            gmv_s.append(_unpad_rows(v_s).reshape(dec_batch, 1, GM_WIDTH))
        xp = mixer_out(xp, o_p, qm, mk_p, mv_p, w_out[l], seq, 512)
        xs = _unpad_rows(mixer_out(xs8, o_s, qms, mk_s, mv_s, w_out[l], SAMPLE_ROWS, SAMPLE_ROWS))
        moe_w = (norm_ffn_g[l], w_router_grp[l], b_router_grp[l], w_router_exp[l], b_router_exp[l],
                 w_gate[l], w_up[l], w_down[l])
        xp = hier_moe_residual(xp, *moe_w, 512, 256)
        xs = _unpad_rows(hier_moe_residual_rows(_pad_rows(xs), *moe_w, SAMPLE_ROWS))

    y_prompt = rmsnorm_rows(xp, final_norm_g, 1024).reshape(batch, seq, d)
    y_sample = rmsnorm_rows(xs, final_norm_g, 32).reshape(dec_batch, 1, d)
    return (y_prompt, y_sample, jnp.stack(sbk_p), jnp.stack(sbv_p), jnp.stack(sbk_s), jnp.stack(sbv_s),
            jnp.stack(memk_p), jnp.stack(memv_p), jnp.stack(gmv_s))
```

```python
import functools

import numpy as np
import jax
import jax.numpy as jnp
from jax import lax
from jax.experimental import pallas as pl
from jax.experimental.pallas import tpu as pltpu

F32 = jnp.float32
BF16 = jnp.bfloat16
I32 = jnp.int32

HEAD_DIM = 64
LANES = 128
SUBLANES = 8
N_SB_HEADS = 12
SB_WIDTH = N_SB_HEADS * HEAD_DIM
N_MEM_HEADS = 4
MEM_WIDTH = N_MEM_HEADS * HEAD_DIM
GM_GROUPS = 4
GM_WIDTH = 768
GM_GROUP_DIM = GM_WIDTH // GM_GROUPS
CHUNK = 128
PAGE_SIZE = 128
N_GROUPS = 4
EXPERTS_PER_GROUP = 8
N_EXPERTS = N_GROUPS * EXPERTS_PER_GROUP
TOP_K = 2
RMS_EPS = 1e-6
ATTN_SCALE = HEAD_DIM ** -0.5

SB_TK = 128
SB_TQ = 512
SB_STEP_HEADS = 12
PAGES_PER_STEP = 16
SAMPLE_ROWS = 2 * SUBLANES
VMEM_LIMIT = 48 * 1024 * 1024
NT_DIMS = (((1,), (1,)), ((), ()))
ISSUE_UNROLL = 8
DMA_QUEUES = 2


def _cparams(n_axes):
    return pltpu.CompilerParams(dimension_semantics=("arbitrary",) * n_axes,
                                vmem_limit_bytes=VMEM_LIMIT)


def _mm(a, b):
    return jnp.dot(a.astype(BF16), b.astype(BF16), preferred_element_type=F32)


def _mm_nt(a, b):
    return lax.dot_general(a.astype(BF16), b.astype(BF16), NT_DIMS, preferred_element_type=F32)


def _split_bf16(x):
    hi = x.astype(BF16)
    lo = (x - hi.astype(F32)).astype(BF16)
    return hi, lo


def _round_bf16(x):
    return x.astype(BF16).astype(F32)


def _rms(x, g):
    return x * lax.rsqrt(jnp.mean(x * x, axis=-1, keepdims=True) + RMS_EPS) * g


def _gelu(x):
    return x * (0.5 * (1.0 + jnp.tanh(np.sqrt(2.0 / np.pi).astype(np.float32) * (x + 0.044715 * (x * x * x)))))


def _log_sigmoid_pair(z):
    l = jnp.log(1.0 + jnp.exp(-jnp.abs(z)))
    return jnp.minimum(z, 0.0) - l, -(jnp.maximum(z, 0.0) + l)


def _cumsum_weights():
    j = np.arange(SB_TK)[:, None]
    s = np.arange(SB_TK)[None, :]
    w = np.concatenate([(j > s).astype(np.float32), np.ones((SB_TK, SB_TK), np.float32)], axis=1)
    return jnp.asarray(np.concatenate([w, w], axis=0), dtype=BF16)


def _later_sums(log_keep, w2):
    hi, lo = _split_bf16(log_keep)
    return jnp.dot(jnp.concatenate([hi, lo], axis=1), w2, preferred_element_type=F32)


def _norm_matmul_kernel(x_ref, g_ref, *refs, n_out):
    h = _rms(x_ref[...], g_ref[...]).astype(BF16)
    for w_ref, o_ref in zip(refs[:n_out], refs[n_out:]):
        o_ref[...] = jnp.dot(h, w_ref[...], preferred_element_type=F32)


def norm_matmul(x, g, ws, tm):
    n, d = x.shape
    tm = min(tm, n)
    assert n % tm == 0
    ws = [w.astype(BF16) for w in ws]
    in_specs = [pl.BlockSpec((tm, d), lambda i: (i, 0)), pl.BlockSpec((1, d), lambda i: (0, 0))]
    in_specs += [pl.BlockSpec(w.shape, lambda i: (0, 0)) for w in ws]
    out_specs = [pl.BlockSpec((tm, w.shape[1]), lambda i: (i, 0)) for w in ws]
    out_shape = [jax.ShapeDtypeStruct((n, w.shape[1]), F32) for w in ws]
    return pl.pallas_call(
        functools.partial(_norm_matmul_kernel, n_out=len(ws)),
        grid=(n // tm,), in_specs=in_specs, out_specs=out_specs, out_shape=out_shape,
        compiler_params=_cparams(1))(x, g.reshape(1, d), *ws)


def _rmsnorm_kernel(x_ref, g_ref, o_ref):
    o_ref[...] = _rms(x_ref[...], g_ref[...])


def rmsnorm_rows(x, g, tm):
    n, d = x.shape
    tm = min(tm, n)
    assert n % tm == 0
    return pl.pallas_call(
        _rmsnorm_kernel, grid=(n // tm,),
        in_specs=[pl.BlockSpec((tm, d), lambda i: (i, 0)), pl.BlockSpec((1, d), lambda i: (0, 0))],
        out_specs=pl.BlockSpec((tm, d), lambda i: (i, 0)),
        out_shape=jax.ShapeDtypeStruct((n, d), F32), compiler_params=_cparams(1))(x, g.reshape(1, d))


def _sb_prompt_kernel(bias_ref, q_ref, k_ref, v_ref, w2_ref, o_ref, kb_ref, vb_ref, acc_ref, later_ref):
    hg = pl.program_id(1)
    qi = pl.program_id(2)

    @pl.when(qi == 0)
    def _():
        kb_ref[...] = k_ref[...].astype(BF16)
        vb_ref[...] = v_ref[...].astype(BF16)

    lane = lax.broadcasted_iota(I32, (SB_TQ, LANES), 1)
    half_masks = (lane < HEAD_DIM, lane >= HEAD_DIM)
    q = q_ref[...] * ATTN_SCALE
    blocks = [slice((h // 2) * LANES, (h // 2 + 1) * LANES) for h in range(SB_STEP_HEADS)]
    qh = [jnp.where(half_masks[h % 2], q[:, blocks[h]], 0.0).astype(BF16) for h in range(SB_STEP_HEADS)]
    bias = [bias_ref[SB_STEP_HEADS * hg + h] for h in range(SB_STEP_HEADS)]
    w2 = w2_ref[...]
    row = lax.broadcasted_iota(I32, (SB_TQ, SB_TQ), 0)
    col = lax.broadcasted_iota(I32, (SB_TQ, SB_TQ), 1)
    strictly_before = col < row
    acc_ref[...] = jnp.zeros_like(acc_ref)
    later_ref[...] = jnp.zeros_like(later_ref)

    def key_step(j, diagonal):
        start = pl.multiple_of(j * SB_TQ, SB_TQ)
        for h in range(SB_STEP_HEADS):
            kj = kb_ref[pl.ds(start, SB_TQ), blocks[h]]
            vj = vb_ref[pl.ds(start, SB_TQ), blocks[h]]
            z = lax.dot_general(qh[h], kj, NT_DIMS, preferred_element_type=F32) + bias[h]
            log_beta = jnp.minimum(z, 0.0) - jnp.log(1.0 + jnp.exp(-jnp.abs(z)))
            log_keep = log_beta - z
            if diagonal:
                log_keep = jnp.where(strictly_before, log_keep, 0.0)
            later = later_ref[h]
            parts = [None] * (SB_TQ // SB_TK)
            for s in reversed(range(SB_TQ // SB_TK)):
                cols = slice(s * SB_TK, (s + 1) * SB_TK)
                sums = _later_sums(log_keep[:, cols], w2)
                parts[s] = jnp.exp(log_beta[:, cols] + sums[:, :SB_TK] + later)
                later = later + sums[:, SB_TK:]
            a = jnp.concatenate(parts, axis=1)
            if diagonal:
                a = jnp.where(strictly_before, a, 0.0)
            acc_ref[h] += jnp.dot(a.astype(BF16), vj, preferred_element_type=F32)
            later_ref[h] = later

    key_step(qi, True)

    def body(t, c):
        key_step(qi - 1 - t, False)
        return c
    lax.fori_loop(0, qi, body, 0)
    for b in range(SB_STEP_HEADS // 2):
        o_ref[:, b * LANES:(b + 1) * LANES] = jnp.where(half_masks[0], acc_ref[2 * b], acc_ref[2 * b + 1])


def sb_attention_prompt(q, k, v, bias, batch, seq):
    n = q.shape[0]
    assert seq % SB_TQ == 0 and SB_TQ % SB_TK == 0 and n == batch * seq and N_SB_HEADS % SB_STEP_HEADS == 0
    nq = seq // SB_TQ
    width = SB_STEP_HEADS * HEAD_DIM
    return pl.pallas_call(
        _sb_prompt_kernel,
        grid=(batch, N_SB_HEADS // SB_STEP_HEADS, nq),
        in_specs=[
            pl.BlockSpec(memory_space=pltpu.SMEM),
            pl.BlockSpec((SB_TQ, width), lambda b, hg, qi: (b * nq + qi, hg)),
            pl.BlockSpec((seq, width), lambda b, hg, qi: (b, hg)),
            pl.BlockSpec((seq, width), lambda b, hg, qi: (b, hg)),
            pl.BlockSpec((2 * SB_TK, 2 * SB_TK), lambda b, hg, qi: (0, 0)),
        ],
        out_specs=pl.BlockSpec((SB_TQ, width), lambda b, hg, qi: (b * nq + qi, hg)),
        out_shape=jax.ShapeDtypeStruct((n, SB_WIDTH), F32),
        scratch_shapes=[pltpu.VMEM((seq, width), BF16), pltpu.VMEM((seq, width), BF16),
                        pltpu.VMEM((SB_STEP_HEADS, SB_TQ, LANES), F32),
                        pltpu.VMEM((SB_STEP_HEADS, SB_TQ, LANES), F32)],
        compiler_params=_cparams(3))(bias, q, k, v, _cumsum_weights())


SB_HEAD_ROWS = 16


def _sb_sample_kernel(pt_ref, q_ref, bias_ref, w2_ref, *refs):
    del pt_ref
    k_refs = refs[:PAGES_PER_STEP]
    v_refs = refs[PAGES_PER_STEP:2 * PAGES_PER_STEP]
    o_ref = refs[2 * PAGES_PER_STEP]
    qcol_ref, z_ref, a_ref, acc_ref, later_ref = refs[2 * PAGES_PER_STEP + 1:]
    j = pl.program_id(1)

    @pl.when(j == 0)
    def _():
        acc_ref[...] = jnp.zeros_like(acc_ref)
        later_ref[...] = jnp.zeros_like(later_ref)
        z_ref[...] = jnp.zeros_like(z_ref)
        qcol_ref[...] = _round_bf16(jnp.broadcast_to(q_ref[0] * ATTN_SCALE, qcol_ref.shape))

    heads = [slice(h * HEAD_DIM, (h + 1) * HEAD_DIM) for h in range(N_SB_HEADS)]
    for i in range(PAGES_PER_STEP):
        for h, rows in enumerate(heads):
            prod = _round_bf16(k_refs[i][0, rows, :]) * qcol_ref[rows, :]
            z_ref[pl.ds(i * SB_HEAD_ROWS + h, 1), :] = jnp.sum(prod, axis=0, keepdims=True)
    log_beta, log_keep = _log_sigmoid_pair(z_ref[...] + bias_ref[...])
    sums = _later_sums(log_keep, w2_ref[...])
    later = later_ref[...]
    for i in range(PAGES_PER_STEP):
        r = slice(i * SB_HEAD_ROWS, (i + 1) * SB_HEAD_ROWS)
        a_ref[r, :] = _round_bf16(jnp.exp(log_beta[r] + sums[r, :PAGE_SIZE] + later))
        later = later + sums[r, PAGE_SIZE:]
    later_ref[...] = later
    for h, rows in enumerate(heads):
        acc = acc_ref[rows, :]
        for i in range(PAGES_PER_STEP):
            acc = acc + _round_bf16(v_refs[i][0, rows, :]) * a_ref[pl.ds(i * SB_HEAD_ROWS + h, 1), :]
        acc_ref[rows, :] = acc

    @pl.when(j == pl.num_programs(1) - 1)
    def _():
        o_ref[0] = jnp.sum(acc_ref[...], axis=1, keepdims=True)


def sb_attention_sample(q, cache_kt, cache_vt, page_table, bias):
    nb, n_pages = page_table.shape
    assert n_pages % PAGES_PER_STEP == 0 and PAGE_SIZE == SB_TK
    steps = n_pages // PAGES_PER_STEP
    bias_rows = jnp.zeros((SB_HEAD_ROWS, PAGE_SIZE), F32).at[:N_SB_HEADS].set(
        jnp.broadcast_to(bias[:, None], (N_SB_HEADS, PAGE_SIZE)))
    bias_rows = jnp.tile(bias_rows, (PAGES_PER_STEP, 1))
    step_rows = PAGES_PER_STEP * SB_HEAD_ROWS

    def page_spec(i):
        return pl.BlockSpec((1, SB_WIDTH, PAGE_SIZE),
                            lambda b, j, pt: (pt[b, n_pages - 1 - (j * PAGES_PER_STEP + i)], 0, 0))

    grid_spec = pltpu.PrefetchScalarGridSpec(
        num_scalar_prefetch=1, grid=(nb, steps),
        in_specs=[pl.BlockSpec((1, SB_WIDTH, 1), lambda b, j, pt: (b, 0, 0)),
                  pl.BlockSpec((step_rows, PAGE_SIZE), lambda b, j, pt: (0, 0)),
                  pl.BlockSpec((2 * SB_TK, 2 * SB_TK), lambda b, j, pt: (0, 0))]
        + [page_spec(i) for i in range(PAGES_PER_STEP)] * 2,
        out_specs=pl.BlockSpec((1, SB_WIDTH, 1), lambda b, j, pt: (b, 0, 0)),
        scratch_shapes=[pltpu.VMEM((SB_WIDTH, PAGE_SIZE), F32),
                        pltpu.VMEM((step_rows, PAGE_SIZE), F32),
                        pltpu.VMEM((step_rows, PAGE_SIZE), F32),
                        pltpu.VMEM((SB_WIDTH, PAGE_SIZE), F32),
                        pltpu.VMEM((SB_HEAD_ROWS, PAGE_SIZE), F32)])
    out = pl.pallas_call(
        _sb_sample_kernel, grid_spec=grid_spec,
        out_shape=jax.ShapeDtypeStruct((nb, SB_WIDTH, 1), F32),
        compiler_params=_cparams(2))(
            page_table, q.reshape(nb, SB_WIDTH, 1), bias_rows, _cumsum_weights(),
            *([cache_kt] * PAGES_PER_STEP), *([cache_vt] * PAGES_PER_STEP))
    return out.reshape(nb, SB_WIDTH)


def _pages_channel_major(cache):
    n_phys = cache.shape[0]
    return jnp.transpose(cache, (0, 2, 3, 1)).reshape(n_phys, SB_WIDTH, PAGE_SIZE)


def _gm_kernel(x_ref, g_ref, wu_ref, wv_ref, wq_ref, vg_ref, ws_ref, bias_ref, s_ref, qm_ref, *v_out, chunk):
    h = _rms(x_ref[...], g_ref[...]).astype(BF16)
    qm_ref[...] = jnp.dot(h, wq_ref[...], preferred_element_type=F32)
    u = _gelu(jnp.dot(h, wu_ref[...], preferred_element_type=F32))
    v = _rms(_gelu(jnp.dot(h, wv_ref[...], preferred_element_type=F32)), vg_ref[...])
    if v_out:
        v_out[0][...] = v
    vb = v.astype(BF16)
    group = lax.broadcasted_iota(I32, (chunk, GM_WIDTH), 1) // GM_GROUP_DIM
    for c in range(v.shape[0] // chunk):
        rows = slice(c * chunk, (c + 1) * chunk)
        mixed = bias_ref[...]
        for g in range(GM_GROUPS):
            mixed = mixed + jnp.where(group == g, jnp.dot(ws_ref[g], vb[rows], preferred_element_type=F32), 0.0)
        s_ref[rows, :] = u[rows] * mixed


def gm_mixer_in(x, g, w_in, vg, ws_full, bias, chunk, tm, emit_v):
    n, d = x.shape
    tm = min(tm, n)
    assert n % tm == 0 and tm % chunk == 0
    bias_full = jnp.repeat(bias.T, GM_GROUP_DIM, axis=1)
    w = w_in.astype(BF16)
    ws = [w[:, :GM_WIDTH], w[:, GM_WIDTH:2 * GM_WIDTH], w[:, 2 * GM_WIDTH:]]
    row = lambda i: (i, 0)
    fixed = lambda i: (0, 0)
    widths = [GM_WIDTH, MEM_WIDTH] + ([GM_WIDTH] if emit_v else [])
    return pl.pallas_call(
        functools.partial(_gm_kernel, chunk=chunk), grid=(n // tm,),
        in_specs=[pl.BlockSpec((tm, d), row), pl.BlockSpec((1, d), fixed)]
        + [pl.BlockSpec(wi.shape, fixed) for wi in ws]
        + [pl.BlockSpec((1, GM_WIDTH), fixed), pl.BlockSpec((GM_GROUPS, chunk, chunk), lambda i: (0, 0, 0)),
           pl.BlockSpec((chunk, GM_WIDTH), fixed)],
        out_specs=[pl.BlockSpec((tm, wd), row) for wd in widths],
        out_shape=[jax.ShapeDtypeStruct((n, wd), F32) for wd in widths],
        compiler_params=_cparams(1))(x, g.reshape(1, d), *ws, vg.reshape(1, GM_WIDTH), ws_full.astype(BF16), bias_full)


def _mixer_out_kernel(x_ref, o_ref, qm_ref, mk_ref, mv_ref, wo_ref, out_ref):
    qm = qm_ref[...] * ATTN_SCALE
    mk = mk_ref[0].astype(BF16)
    mv = mv_ref[0].astype(BF16)
    head = lax.broadcasted_iota(I32, qm.shape, 1) // HEAD_DIM
    o_mem = jnp.zeros(qm.shape, F32)
    for h in range(N_MEM_HEADS):
        own = head == h
        s = _mm_nt(jnp.where(own, qm, 0.0), mk)
        p = jnp.exp(s - jnp.max(s, axis=-1, keepdims=True))
        p = p / jnp.sum(p, axis=-1, keepdims=True)
        o_mem = jnp.where(own, _mm(p, mv), o_mem)
    wo = wo_ref[...]
    width = o_ref.shape[1]
    out_ref[...] = x_ref[...] + _mm(o_ref[...], wo[:width]) + _mm(o_mem, wo[width:])


def mixer_out(x, o_mix, qm, mk, mv, wo, rows_per_batch, tm):
    n, d = x.shape
    tm = min(tm, rows_per_batch)
    assert rows_per_batch % tm == 0 and n % tm == 0
    per = rows_per_batch // tm
    wo = wo.astype(BF16)
    row = lambda i: (i, 0)
    mem = lambda i: (i // per, 0, 0)
    return pl.pallas_call(
        _mixer_out_kernel, grid=(n // tm,),
        in_specs=[pl.BlockSpec((tm, d), row), pl.BlockSpec((tm, o_mix.shape[1]), row),
                  pl.BlockSpec((tm, MEM_WIDTH), row),
                  pl.BlockSpec((1,) + mk.shape[1:], mem), pl.BlockSpec((1,) + mv.shape[1:], mem),
                  pl.BlockSpec(wo.shape, lambda i: (0, 0))],
        out_specs=pl.BlockSpec((tm, d), row),
        out_shape=jax.ShapeDtypeStruct((n, d), F32),
        compiler_params=_cparams(1))(x, o_mix, qm, mk, mv, wo)


ROUTE_LANES = LANES
REC = SUBLANES


def _strided(ref, first, count):
    return ref.at[pl.ds(first, count, stride=REC), :]


def _route_kernel(x_ref, g_ref, w_ref, b_ref, tri_ref, rec_ref, re_ref, rw_ref, cnt_ref, seen_ref):
    t = _rms(x_ref[...], g_ref[...])
    tm, d = t.shape
    logits = _mm(t, w_ref[...]) + b_ref[...]
    lane = lax.broadcasted_iota(I32, logits.shape, 1).astype(F32)
    neg = jnp.float32(-jnp.inf)
    far = jnp.float32(ROUTE_LANES)

    def first_max(vals):
        m = jnp.max(vals, axis=-1, keepdims=True)
        return m, jnp.min(jnp.where(vals == m, lane, far), axis=-1, keepdims=True)

    is_group = lane < N_GROUPS
    gmax, gsel = first_max(jnp.where(is_group, logits, neg))
    p_grp = 1.0 / jnp.sum(jnp.where(is_group, jnp.exp(logits - gmax), 0.0), axis=-1, keepdims=True)
    lo = N_GROUPS + gsel * EXPERTS_PER_GROUP
    cand = jnp.where((lane >= lo) & (lane < lo + EXPERTS_PER_GROUP), logits, neg)
    m1, i1 = first_max(cand)
    m2, i2 = first_max(jnp.where(lane == i1, neg, cand))
    e2 = jnp.exp(m2 - m1)
    den = 1.0 + e2
    w1 = (1.0 / den) * p_grp
    w2 = (e2 / den) * p_grp
    ex1 = (i1 - N_GROUPS).astype(I32)
    ex2 = (i2 - N_GROUPS).astype(I32)

    @pl.when(pl.program_id(0) == 0)
    def _():
        seen_ref[...] = jnp.zeros_like(seen_ref)

    pick1 = lane == i1
    pick2 = lane == i2
    chosen = jnp.where(pick1 | pick2, 1.0, 0.0)
    before = jnp.dot(tri_ref[...], chosen.astype(BF16), preferred_element_type=F32) + seen_ref[...]
    rank1 = jnp.sum(jnp.where(pick1, before, 0.0), axis=-1, keepdims=True).astype(I32)
    rank2 = jnp.sum(jnp.where(pick2, before, 0.0), axis=-1, keepdims=True).astype(I32)
    seen_ref[...] += jnp.sum(chosen, axis=0, keepdims=True)
    cnt_ref[...] = seen_ref[...]
    re_ref[...] = jnp.where(lane == 0, ex1, jnp.where(lane == 1, ex2,
                            jnp.where(lane == 2, rank1, jnp.where(lane == 3, rank2, 0))))

    rw_ref[...] = jnp.where(lane == 0, w1, jnp.where(lane == 1, w2, 0.0))
    for j in range(d // LANES):
        _strided(rec_ref, j, tm)[...] = t[:, j * LANES:(j + 1) * LANES]


def moe_route(x, g, wg, bg, wr, br, tm):
    n, d = x.shape
    tm = min(tm, n)
    assert n % tm == 0 and d == REC * LANES
    pad = ROUTE_LANES - N_GROUPS - N_EXPERTS
    w = jnp.concatenate([wg, wr, jnp.zeros((d, pad), F32)], axis=1).astype(BF16)
    b = jnp.concatenate([bg, br, jnp.zeros((pad,), F32)]).reshape(1, ROUTE_LANES)
    earlier_rows = jnp.asarray(np.tril(np.ones((tm, tm), np.float32), -1), dtype=BF16)
    row = lambda i: (i, 0)
    fixed = lambda i: (0, 0)
    return pl.pallas_call(
        _route_kernel, grid=(n // tm,),
        in_specs=[pl.BlockSpec((tm, d), row), pl.BlockSpec((1, d), fixed),
                  pl.BlockSpec((d, ROUTE_LANES), fixed), pl.BlockSpec((1, ROUTE_LANES), fixed),
                  pl.BlockSpec((tm, tm), fixed)],
        out_specs=[pl.BlockSpec((tm * REC, LANES), row), pl.BlockSpec((tm, ROUTE_LANES), row),
                   pl.BlockSpec((tm, ROUTE_LANES), row), pl.BlockSpec((1, ROUTE_LANES), fixed)],
        out_shape=[jax.ShapeDtypeStruct((n * REC, LANES), F32), jax.ShapeDtypeStruct((n, ROUTE_LANES), I32),
                   jax.ShapeDtypeStruct((n, ROUTE_LANES), F32), jax.ShapeDtypeStruct((1, ROUTE_LANES), F32)],
        scratch_shapes=[pltpu.VMEM((1, ROUTE_LANES), F32)],
        compiler_params=_cparams(1))(x, g.reshape(1, d), w, b, earlier_rows)


def _record(ref, index):
    return ref.at[pl.ds(pl.multiple_of(index * REC, REC), REC)]


def _tile_index_specs(nt, width):
    cur = pl.BlockSpec((1, 1, width), lambda i: (i, 0, 0), memory_space=pltpu.SMEM)
    nxt = pl.BlockSpec((1, 1, width), lambda i: (jnp.minimum(i + 1, nt - 1), 0, 0), memory_space=pltpu.SMEM)
    return [cur, nxt]


def _dispatch_kernel(fill_ref, pos_ref, rec_ref, dst_hbm, stage, sem, *, tokens):
    i = pl.program_id(0)
    last = pl.num_programs(0) - 1
    slot = i % 2

    def tile_copy(src, first_record, s):
        return pltpu.make_async_copy(src, dst_hbm.at[pl.ds(pl.multiple_of(first_record * REC, REC), tokens * REC)],
                                     sem.at[s])

    @pl.when(i == 0)
    def _():
        stage[1] = jnp.zeros(stage.shape[1:], stage.dtype)
        for e in range(fill_ref.shape[0]):
            @pl.when(fill_ref[e] >= 0)
            def _():
                tile_copy(stage.at[1], fill_ref[e], 2).start()
        for e in range(fill_ref.shape[0]):
            @pl.when(fill_ref[e] >= 0)
            def _():
                tile_copy(stage.at[1], fill_ref[e], 2).wait()

    def wait_step(s):
        for _ in range(TOP_K):
            tile_copy(stage.at[s], 0, s).wait()

    @pl.when(i >= 2)
    def _():
        wait_step(slot)

    stage[slot] = rec_ref[...]

    def body(r, c):
        src = _record(stage.at[slot], r)
        for k in range(TOP_K):
            pltpu.make_async_copy(src, _record(dst_hbm, pos_ref[0, 0, k * tokens + r]),
                                  sem.at[slot]).start(priority=k % DMA_QUEUES)
        return c
    lax.fori_loop(0, tokens, body, 0, unroll=ISSUE_UNROLL)

    @pl.when(i == last)
    def _():
        wait_step(slot)

    @pl.when((i == last) & (i >= 1))
    def _():
        wait_step(1 - slot)


def dispatch_records(rec, pos, fill_first, total, tokens):
    n = pos.shape[0]
    assert n % tokens == 0 and total % tokens == 0
    nt = n // tokens
    idx3 = pos.reshape(nt, tokens, TOP_K).transpose(0, 2, 1).reshape(nt, 1, TOP_K * tokens)
    grid_spec = pltpu.PrefetchScalarGridSpec(
        num_scalar_prefetch=1, grid=(nt,),
        in_specs=[pl.BlockSpec((1, 1, TOP_K * tokens), lambda i, fill: (i, 0, 0), memory_space=pltpu.SMEM),
                  pl.BlockSpec((tokens * REC, LANES), lambda i, fill: (i, 0))],
        out_specs=pl.BlockSpec(memory_space=pl.ANY),
        scratch_shapes=[pltpu.VMEM((2, tokens * REC, LANES), F32), pltpu.SemaphoreType.DMA((3,))])
    return pl.pallas_call(
        functools.partial(_dispatch_kernel, tokens=tokens), grid_spec=grid_spec,
        out_shape=jax.ShapeDtypeStruct((total * REC, LANES), F32),
        compiler_params=_cparams(1))(fill_first, idx3, rec)


def _expert_kernel(te_ref, used_ref, rec_ref, wg_ref, wu_ref, wd_ref, y_ref, wg, wu, wd, *, tm):
    i = pl.program_id(0)
    prev = te_ref[jnp.maximum(i - 1, 0)]

    @pl.when((i == 0) | (te_ref[i] != prev))
    def _():
        wg[...] = wg_ref[0].astype(BF16)
        wu[...] = wu_ref[0].astype(BF16)
        wd[...] = wd_ref[0].astype(BF16)

    @pl.when(i < used_ref[0])
    def _():
        t = jnp.concatenate([_strided(rec_ref, j, tm)[...] for j in range(REC)], axis=1).astype(BF16)
        a = jnp.dot(t, wg[...], preferred_element_type=F32)
        c = jnp.dot(t, wu[...], preferred_element_type=F32)
        y = _mm(a * (1.0 / (1.0 + jnp.exp(-a))) * c, wd[...])
        for j in range(y.shape[1] // LANES):
            _strided(y_ref, j, tm)[...] = y[:, j * LANES:(j + 1) * LANES]

    @pl.when(i >= used_ref[0])
    def _():
        y_ref[...] = jnp.zeros_like(y_ref)


def moe_experts(rec_sorted, tile_expert, n_used, w_gate, w_up, w_down, tm):
    p = rec_sorted.shape[0] // REC
    _, d, f = w_gate.shape
    assert p % tm == 0 and d == REC * LANES and f % LANES == 0
    grid_spec = pltpu.PrefetchScalarGridSpec(
        num_scalar_prefetch=2, grid=(p // tm,),
        in_specs=[pl.BlockSpec((tm * REC, LANES), lambda i, te, nu: (jnp.minimum(i, nu[0] - 1), 0)),
                  pl.BlockSpec((1, d, f), lambda i, te, nu: (te[i], 0, 0)),
                  pl.BlockSpec((1, d, f), lambda i, te, nu: (te[i], 0, 0)),
                  pl.BlockSpec((1, f, d), lambda i, te, nu: (te[i], 0, 0))],
        out_specs=pl.BlockSpec((tm * REC, LANES), lambda i, te, nu: (i, 0)),
        scratch_shapes=[pltpu.VMEM((d, f), BF16), pltpu.VMEM((d, f), BF16), pltpu.VMEM((f, d), BF16)])
    return pl.pallas_call(
        functools.partial(_expert_kernel, tm=tm), grid_spec=grid_spec,
        out_shape=jax.ShapeDtypeStruct((p * REC, LANES), F32),
        compiler_params=_cparams(1))(tile_expert, n_used, rec_sorted, w_gate, w_up, w_down)


def _combine_kernel(idx_ref, nxt_ref, x_ref, rw_ref, y_hbm, out_ref, buf, sem, *, rows):
    i = pl.program_id(0)
    slot = i % 2
    count = TOP_K * rows

    def issue(ref, s):
        def body(p, c):
            for q in range(DMA_QUEUES):
                r = p * DMA_QUEUES + q
                pltpu.make_async_copy(_record(y_hbm, ref[0, 0, r]), _record(buf.at[s], r),
                                      sem.at[s]).start(priority=q)
            return c
        lax.fori_loop(0, count // DMA_QUEUES, body, 0, unroll=ISSUE_UNROLL)

    @pl.when(i == 0)
    def _():
        issue(idx_ref, 0)

    @pl.when(i + 1 < pl.num_programs(0))
    def _():
        issue(nxt_ref, 1 - slot)

    pltpu.make_async_copy(y_hbm.at[pl.ds(0, count * REC)], buf.at[slot], sem.at[slot]).wait()
    rw = rw_ref[...]
    w1 = jnp.broadcast_to(rw[:, 0:1], (rows, LANES))
    w2 = jnp.broadcast_to(rw[:, 1:2], (rows, LANES))
    for j in range(REC):
        cols = slice(j * LANES, (j + 1) * LANES)
        first = buf[slot, pl.ds(j, rows, stride=REC), :]
        second = buf[slot, pl.ds(rows * REC + j, rows, stride=REC), :]
        out_ref[:, cols] = x_ref[:, cols] + (w1 * first + w2 * second)


def moe_combine(x, y_rec, pos, route_w, rows):
    n, d = x.shape
    rows = min(rows, n)
    assert n % rows == 0 and y_rec.shape[0] >= TOP_K * rows * REC and d == REC * LANES
    nt = n // rows
    idx3 = pos.reshape(nt, rows, TOP_K).transpose(0, 2, 1).reshape(nt, 1, TOP_K * rows)
    return pl.pallas_call(
        functools.partial(_combine_kernel, rows=rows), grid=(nt,),
        in_specs=_tile_index_specs(nt, TOP_K * rows)
        + [pl.BlockSpec((rows, d), lambda i: (i, 0)), pl.BlockSpec((rows, ROUTE_LANES), lambda i: (i, 0)),
           pl.BlockSpec(memory_space=pl.ANY)],
        out_specs=pl.BlockSpec((rows, d), lambda i: (i, 0)),
        out_shape=jax.ShapeDtypeStruct((n, d), F32),
        scratch_shapes=[pltpu.VMEM((2, TOP_K * rows * REC, LANES), F32), pltpu.SemaphoreType.DMA((2,))],
        compiler_params=_cparams(1))(idx3, idx3, x, route_w, y_rec)


def _moe_plan(route_e, seen, tm):
    n = route_e.shape[0]
    experts = jnp.arange(N_EXPERTS, dtype=I32)
    counts = seen[0, N_GROUPS:N_GROUPS + N_EXPERTS].astype(I32)
    padded = ((counts + tm - 1) // tm) * tm
    pad_end = jnp.cumsum(padded)
    pad_start = pad_end - padded
    e = route_e[:, :TOP_K]
    rank = route_e[:, TOP_K:2 * TOP_K]
    pos = jnp.sum(jnp.where(e[:, :, None] == experts, pad_start, 0), axis=-1) + rank
    total = ((TOP_K * n + tm - 1) // tm) * tm + N_EXPERTS * tm
    tile_start = jnp.arange(total // tm, dtype=I32) * tm
    tile_expert = jnp.minimum(jnp.sum((tile_start[:, None] >= pad_end[None, :]).astype(I32), axis=1), N_EXPERTS - 1)
    n_used = (pad_end[-1:] // tm).astype(I32)
    last_tiles = jnp.where(padded > 0, pad_end - tm, -1)
    spare = (total - N_EXPERTS * tm) // tm + jnp.arange(N_EXPERTS, dtype=I32)
    spare_tiles = jnp.where(spare >= n_used[0], spare * tm, -1)
    fill_first = jnp.concatenate([last_tiles, spare_tiles]).astype(I32)
    return pos, tile_expert, n_used, fill_first, total


def hier_moe_residual(x, g, wg, bg, wr, br, w_gate, w_up, w_down, tm_tokens, tm_expert):
    rec, route_e, route_w, seen = moe_route(x, g, wg, bg, wr, br, tm_tokens)
    pos, tile_expert, n_used, fill_first, total = _moe_plan(route_e, seen, tm_expert)
    rec_sorted = dispatch_records(rec, pos, fill_first, total, tm_expert)
    y_rec = moe_experts(rec_sorted, tile_expert, n_used, w_gate, w_up, w_down, tm_expert)
    return moe_combine(x, y_rec, pos, route_w, tm_expert)


def _moe_rows_kernel(e_ref, w_ref, x_ref, g_ref, wg_ref, wu_ref, wd_ref, o_ref):
    a_id = pl.program_id(0) * TOP_K + pl.program_id(1)
    x = x_ref[...]
    t = _rms(x, g_ref[...]).astype(BF16)
    a = jnp.dot(t, wg_ref[0].astype(BF16), preferred_element_type=F32)
    c = jnp.dot(t, wu_ref[0].astype(BF16), preferred_element_type=F32)
    y = _mm(a * (1.0 / (1.0 + jnp.exp(-a))) * c * w_ref[a_id], wd_ref[0])

    @pl.when(pl.program_id(1) == 0)
    def _():
        o_ref[...] = x + y

    @pl.when(pl.program_id(1) > 0)
    def _():
        o_ref[...] += y


def hier_moe_residual_rows(x, g, wg, bg, wr, br, w_gate, w_up, w_down, rows_per_token):
    n, d = x.shape
    nt = n // rows_per_token
    _, f = w_gate.shape[1:]
    _, route_e, route_w, _ = moe_route(x, g, wg, bg, wr, br, n)
    e = route_e.reshape(nt, rows_per_token, ROUTE_LANES)[:, 0, :TOP_K].reshape(-1)
    w = route_w.reshape(nt, rows_per_token, ROUTE_LANES)[:, 0, :TOP_K].reshape(-1)
    expert = lambda i, k, e_ref: (e_ref[i * TOP_K + k], 0, 0)
    grid_spec = pltpu.PrefetchScalarGridSpec(
        num_scalar_prefetch=1, grid=(nt, TOP_K),
        in_specs=[pl.BlockSpec(memory_space=pltpu.SMEM),
                  pl.BlockSpec((rows_per_token, d), lambda i, k, e_ref: (i, 0)),
                  pl.BlockSpec((1, d), lambda i, k, e_ref: (0, 0)),
                  pl.BlockSpec((1, d, f), expert), pl.BlockSpec((1, d, f), expert), pl.BlockSpec((1, f, d), expert)],
        out_specs=pl.BlockSpec((rows_per_token, d), lambda i, k, e_ref: (i, 0)))
    return pl.pallas_call(
        _moe_rows_kernel, grid_spec=grid_spec,
        out_shape=jax.ShapeDtypeStruct((n, d), F32),
        compiler_params=_cparams(2))(e, w, x, g.reshape(1, d), w_gate, w_up, w_down)


def _pad_rows(x):
    b, c = x.shape
    return jnp.zeros((b, SAMPLE_ROWS, c), x.dtype).at[:, 0].set(x).reshape(b * SAMPLE_ROWS, c)


def _unpad_rows(x):
    return x.reshape(-1, SAMPLE_ROWS, x.shape[1])[:, 0]


def kernel(x_prompt, x_sample, mem_prompt, cache_sb_k, cache_sb_v, cache_mem_k, cache_mem_v, page_table, norm_mix_g, norm_ffn_g, norm_mem_g, w_mem_kv, w_in_sb, sb_bias, w_in_gm, gm_norm_g, gm_ws_tril, gm_b, w_out, w_router_grp, b_router_grp, w_router_exp, b_router_exp, w_gate, w_up, w_down, final_norm_g):
    batch, seq, d = x_prompt.shape
    dec_batch, dec_seq, _ = x_sample.shape
    assert dec_seq == 1
    n_mem = mem_prompt.shape[1]
    depth = w_out.shape[0]
    rows, cols = np.tril_indices(CHUNK)

    xp = x_prompt.reshape(batch * seq, d)
    xs = x_sample.reshape(dec_batch, d)
    memp = mem_prompt.reshape(batch * n_mem, d)
    sbk_p, sbv_p, sbk_s, sbv_s, memk_p, memv_p, gmv_s = [], [], [], [], [], [], []

    for l in range(depth):
        i = l // 2
        mk_p, mv_p = norm_matmul(memp, norm_mem_g[l], [w_mem_kv[l][:, :MEM_WIDTH], w_mem_kv[l][:, MEM_WIDTH:]], 512)
        memk_p.append(mk_p.reshape(batch, n_mem, N_MEM_HEADS, HEAD_DIM))
        memv_p.append(mv_p.reshape(batch, n_mem, N_MEM_HEADS, HEAD_DIM))
        mk_p = mk_p.reshape(batch, n_mem, MEM_WIDTH)
        mv_p = mv_p.reshape(batch, n_mem, MEM_WIDTH)
        mk_s = cache_mem_k[l].reshape(dec_batch, n_mem, MEM_WIDTH)
        mv_s = cache_mem_v[l].reshape(dec_batch, n_mem, MEM_WIDTH)
        xs8 = _pad_rows(xs)
        if l % 2 == 0:
            w = w_in_sb[i]
            ws = [w[:, :SB_WIDTH], w[:, SB_WIDTH:2 * SB_WIDTH], w[:, 2 * SB_WIDTH:3 * SB_WIDTH], w[:, 3 * SB_WIDTH:]]
            q, k, v, qm = norm_matmul(xp, norm_mix_g[l], ws, 512)
            sbk_p.append(k.reshape(batch, seq, N_SB_HEADS, HEAD_DIM))
            sbv_p.append(v.reshape(batch, seq, N_SB_HEADS, HEAD_DIM))
            o_p = sb_attention_prompt(q, k, v, sb_bias[i], batch, seq)
            qs, ks, vs, qms = norm_matmul(xs8, norm_mix_g[l], ws, 512)
            sbk_s.append(_unpad_rows(ks).reshape(dec_batch, 1, N_SB_HEADS, HEAD_DIM))
            sbv_s.append(_unpad_rows(vs).reshape(dec_batch, 1, N_SB_HEADS, HEAD_DIM))
            o_s = sb_attention_sample(_unpad_rows(qs), _pages_channel_major(cache_sb_k[i]),
                                      _pages_channel_major(cache_sb_v[i]), page_table, sb_bias[i])
            o_s = _pad_rows(o_s)
        else:
            ws_full = jnp.zeros((GM_GROUPS, CHUNK, CHUNK), F32).at[:, rows, cols].set(gm_ws_tril[i])
            o_p, qm = gm_mixer_in(xp, norm_mix_g[l], w_in_gm[i], gm_norm_g[i], ws_full, gm_b[i], CHUNK, 512, False)
            o_s, qms, v_s = gm_mixer_in(xs8, norm_mix_g[l], w_in_gm[i], gm_norm_g[i],
                                        ws_full[:, :SAMPLE_ROWS, :SAMPLE_ROWS], gm_b[i][:, :SAMPLE_ROWS],
                                        SAMPLE_ROWS, 512, True)
            gmv_s.append(_unpad_rows(v_s).reshape(dec_batch, 1, GM_WIDTH))
        xp = mixer_out(xp, o_p, qm, mk_p, mv_p, w_out[l], seq, 512)
        xs = _unpad_rows(mixer_out(xs8, o_s, qms, mk_s, mv_s, w_out[l], SAMPLE_ROWS, SAMPLE_ROWS))
        moe_w = (norm_ffn_g[l], w_router_grp[l], b_router_grp[l], w_router_exp[l], b_router_exp[l],
                 w_gate[l], w_up[l], w_down[l])
        xp = hier_moe_residual(xp, *moe_w, 512, 256)
        xs = _unpad_rows(hier_moe_residual_rows(_pad_rows(xs), *moe_w, SAMPLE_ROWS))

    y_prompt = rmsnorm_rows(xp, final_norm_g, 1024).reshape(batch, seq, d)
    y_sample = rmsnorm_rows(xs, final_norm_g, 32).reshape(dec_batch, 1, d)
    return (y_prompt, y_sample, jnp.stack(sbk_p), jnp.stack(sbv_p), jnp.stack(sbk_s), jnp.stack(sbv_s),
            jnp.stack(memk_p), jnp.stack(memv_p), jnp.stack(gmv_s))
```
